```python
import math
import jax, jax.numpy as jnp
from jax import lax
import numpy as np

D_MODEL = 1024
BATCH = 2
SEQ = 8192
DEPTH = 2

N_MIXERS = 2
HEAD_DIM = 64
V_DIM = 2 * HEAD_DIM
N_HEADS = D_MODEL // V_DIM
QK_WIDTH = N_HEADS * 2 * HEAD_DIM
V_WIDTH = N_HEADS * V_DIM
ROT_DIM = HEAD_DIM // 4
ROPE_THETA = 500000.0
Q_BLOCK = 128
CONV_WIDTH = 3
CONV_DIM = D_MODEL
D_FF = 4 * D_MODEL
EPS = 1e-6
N_ATTN = (DEPTH + 1) // 2
N_CONV = DEPTH // 2

kernel_name = "hybrid_diffattn_shortconv_sandwich"


def rmsnorm(x, g):
    xf = x.astype(jnp.float32)
    y = xf * lax.rsqrt(jnp.mean(xf * xf, axis=-1, keepdims=True) + EPS)
    return (y * g.astype(jnp.float32)).astype(x.dtype)


def rope_tables(seq_len):
    pos = jnp.arange(seq_len, dtype=jnp.float32)
    inv_freq = ROPE_THETA ** (-jnp.arange(0, ROT_DIM, 2, dtype=jnp.float32) / ROT_DIM)
    ang = pos[:, None] * inv_freq[None, :]
    return jnp.cos(ang), jnp.sin(ang)


def apply_partial_rope(t, cos, sin):
    half = ROT_DIM // 2
    x1 = t[..., :half].astype(jnp.float32)
    x2 = t[..., half:ROT_DIM].astype(jnp.float32)
    c = cos[None, :, None, None, :]
    s = sin[None, :, None, None, :]
    rot = jnp.concatenate([x1 * c - x2 * s, x2 * c + x1 * s], axis=-1).astype(t.dtype)
    return jnp.concatenate([rot, t[..., ROT_DIM:]], axis=-1)


def lambda_init_fn(layer_idx):
    return 0.8 - 0.6 * math.exp(-0.3 * layer_idx)


def diff_attention(x, w_qkv, w_o, lq1, lk1, lq2, lk2, subln_g, lambda_init):
    B, S, _ = x.shape
    qkv = x @ w_qkv
    q = qkv[..., :QK_WIDTH].reshape(B, S, N_HEADS, 2, HEAD_DIM)
    k = qkv[..., QK_WIDTH:2 * QK_WIDTH].reshape(B, S, N_HEADS, 2, HEAD_DIM)
    v = qkv[..., 2 * QK_WIDTH:].reshape(B, S, N_HEADS, V_DIM)
    cos, sin = rope_tables(S)
    q = apply_partial_rope(q, cos, sin) * (HEAD_DIM ** -0.5)
    k = apply_partial_rope(k, cos, sin)
    lam = (jnp.exp(jnp.sum(lq1.astype(jnp.float32) * lk1.astype(jnp.float32)))
           - jnp.exp(jnp.sum(lq2.astype(jnp.float32) * lk2.astype(jnp.float32)))
           + lambda_init)
    n_blocks = S // Q_BLOCK
    qb = q.reshape(B, n_blocks, Q_BLOCK, N_HEADS, 2, HEAD_DIM).transpose(1, 0, 2, 3, 4, 5)
    k32 = k.astype(jnp.float32)
    key_pos = jnp.arange(S)

    def block(args):
        q_i, i = args
        s = jnp.einsum('bqhcd,bkhcd->bhcqk', q_i.astype(jnp.float32), k32)
        q_pos = i * Q_BLOCK + jnp.arange(Q_BLOCK)
        mask = key_pos[None, :] <= q_pos[:, None]
        s = jnp.where(mask, s, -jnp.inf)
        p = jax.nn.softmax(s, axis=-1)
        a = p[:, :, 0] - lam * p[:, :, 1]
        return jnp.einsum('bhqk,bkhe->bqhe', a.astype(v.dtype), v)

    o = lax.map(block, (qb, jnp.arange(n_blocks)))
    o = o.transpose(1, 0, 2, 3, 4).reshape(B, S, N_HEADS, V_DIM)
    o = rmsnorm(o, subln_g) * (1.0 - lambda_init)
    return o.reshape(B, S, V_WIDTH) @ w_o


def short_conv(x, w_in, conv_w, w_out):
    h = x @ w_in
    b_gate = h[..., :CONV_DIM]
    c_gate = h[..., CONV_DIM:2 * CONV_DIM]
    u = c_gate * h[..., 2 * CONV_DIM:]
    up = jnp.pad(u, ((0, 0), (CONV_WIDTH - 1, 0), (0, 0)))
    S = x.shape[1]
    y = (conv_w[0] * up[:, 0:S] + conv_w[1] * up[:, 1:S + 1] + conv_w[2] * up[:, 2:S + 2])
    return (b_gate * y) @ w_out


def sq_relu_mlp(x, w_up, w_down):
    return jnp.square(jax.nn.relu(x @ w_up)) @ w_down


def setup_inputs(seed: int = 0) -> dict:
    key = jax.random.key(seed)
    ks = jax.random.split(key, 20)
    f32 = jnp.float32
    D = D_MODEL

    def nrm(k, shape, scale):
        return jax.random.normal(k, shape, f32) * scale

    return {
        "x": nrm(ks[0], (BATCH, SEQ, D), 1.0),
        "attn_w_qkv": nrm(ks[1], (N_ATTN, D, 2 * QK_WIDTH + V_WIDTH), D ** -0.5),
        "attn_w_o": nrm(ks[2], (N_ATTN, V_WIDTH, D), V_WIDTH ** -0.5),
        "attn_lambda_q1": nrm(ks[3], (N_ATTN, HEAD_DIM), 0.1),
        "attn_lambda_k1": nrm(ks[4], (N_ATTN, HEAD_DIM), 0.1),
        "attn_lambda_q2": nrm(ks[5], (N_ATTN, HEAD_DIM), 0.1),
        "attn_lambda_k2": nrm(ks[6], (N_ATTN, HEAD_DIM), 0.1),
        "attn_subln_g": 1.0 + nrm(ks[7], (N_ATTN, V_DIM), 0.02),
        "conv_w_in": nrm(ks[8], (N_CONV, D, 3 * CONV_DIM), D ** -0.5),
        "conv_w": nrm(ks[9], (N_CONV, CONV_WIDTH, CONV_DIM), CONV_WIDTH ** -0.5),
        "conv_w_out": nrm(ks[10], (N_CONV, CONV_DIM, D), CONV_DIM ** -0.5),
        "mlp_w_up": nrm(ks[11], (DEPTH, D, D_FF), D ** -0.5),
        "mlp_w_down": nrm(ks[12], (DEPTH, D_FF, D), D_FF ** -0.5),
        "norm_mixer_pre": 1.0 + nrm(ks[13], (DEPTH, D), 0.02),
        "norm_mixer_post": 1.0 + nrm(ks[14], (DEPTH, D), 0.02),
        "norm_mlp_pre": 1.0 + nrm(ks[15], (DEPTH, D), 0.02),
        "norm_mlp_post": 1.0 + nrm(ks[16], (DEPTH, D), 0.02),
    }


def reference(x, attn_w_qkv, attn_w_o, attn_lambda_q1, attn_lambda_k1, attn_lambda_q2,
              attn_lambda_k2, attn_subln_g, conv_w_in, conv_w, conv_w_out, mlp_w_up,
              mlp_w_down, norm_mixer_pre, norm_mixer_post, norm_mlp_pre, norm_mlp_post):
    for i in range(DEPTH):
        h = rmsnorm(x, norm_mixer_pre[i])
        j = i // N_MIXERS
        if i % N_MIXERS == 0:
            m = diff_attention(h, attn_w_qkv[j], attn_w_o[j], attn_lambda_q1[j],
                               attn_lambda_k1[j], attn_lambda_q2[j], attn_lambda_k2[j],
                               attn_subln_g[j], lambda_init_fn(i))
        else:
            m = short_conv(h, conv_w_in[j], conv_w[j], conv_w_out[j])
        x = x + rmsnorm(m, norm_mixer_post[i])
        h = rmsnorm(x, norm_mlp_pre[i])
        x = x + rmsnorm(sq_relu_mlp(h, mlp_w_up[i], mlp_w_down[i]), norm_mlp_post[i])
    return x
```

```python
import functools
import math

import jax
import jax.numpy as jnp
from jax import lax
from jax.experimental import pallas as pl
from jax.experimental.pallas import tpu as pltpu

D_MODEL = 1024
HEAD_DIM = 64
V_DIM = 2 * HEAD_DIM
N_HEADS = D_MODEL // V_DIM
QK_WIDTH = N_HEADS * 2 * HEAD_DIM
ROT_DIM = HEAD_DIM // 4
ROPE_THETA = 500000.0
CONV_WIDTH = 3
D_FF = 4 * D_MODEL
EPS = 1e-6

V7X_LANES = 128
V7X_SUBLANES = 8
V7X_VMEM_LIMIT_BYTES = 56 * 1024 * 1024

TOKEN_TILE = 512
ATTN_TILE = 512
FF_CHUNK = 1024
LOG2E = 1.4426950408889634

_BF16 = jnp.bfloat16
_F32 = jnp.float32


def _lambda_init(layer_idx):
    return 0.8 - 0.6 * math.exp(-0.3 * layer_idx)


def _rms(x, g):
    return x * lax.rsqrt(jnp.mean(x * x, axis=-1, keepdims=True) + EPS) * g


def _dot(a, b):
    return jnp.dot(a, b, preferred_element_type=_F32)


def _resident(shape):
    return pl.BlockSpec(shape, lambda *_: (0,) * len(shape),
                        pipeline_mode=pl.Buffered(1))


def _qkv_rope_kernel(x_ref, g_ref, w_ref, ca_ref, cm_ref, cp_ref,
                     q_ref, kt_ref, v_ref):
    h = _rms(x_ref[...], g_ref[...]).astype(_BF16)
    qkv = _dot(h, w_ref[...])
    ca, cm, cp = ca_ref[...], cm_ref[...], cp_ref[...]

    def rope(t):
        return (t * ca + pltpu.roll(t, V7X_LANES - ROT_DIM // 2, 1) * cm
                + pltpu.roll(t, ROT_DIM // 2, 1) * cp)

    q_scale = (HEAD_DIM ** -0.5) * LOG2E
    for hd in range(N_HEADS):
        lo, hi = hd * V_DIM, (hd + 1) * V_DIM
        q_ref[:, lo:hi] = (rope(qkv[:, lo:hi]) * q_scale).astype(_BF16)
        kh = rope(qkv[:, QK_WIDTH + lo:QK_WIDTH + hi])
        kt_ref[lo:hi, :] = kh.T.astype(_BF16)
    v_ref[...] = qkv[:, 2 * QK_WIDTH:].astype(_BF16)


def _qkv_rope_call(x, g, w_qkv, ca, cm, cp):
    B, S, D = x.shape
    tm = TOKEN_TILE
    n_w = w_qkv.shape[1]
    return pl.pallas_call(
        _qkv_rope_kernel,
        grid=(B, S // tm),
        in_specs=[
            pl.BlockSpec((None, tm, D), lambda b, i: (b, i, 0)),
            _resident((1, D)),
            _resident((D, n_w)),
            pl.BlockSpec((tm, V_DIM), lambda b, i: (i, 0)),
            pl.BlockSpec((tm, V_DIM), lambda b, i: (i, 0)),
            pl.BlockSpec((tm, V_DIM), lambda b, i: (i, 0)),
        ],
        out_specs=[
            pl.BlockSpec((None, tm, QK_WIDTH), lambda b, i: (b, i, 0)),
            pl.BlockSpec((None, QK_WIDTH, tm), lambda b, i: (b, 0, i)),
            pl.BlockSpec((None, tm, D), lambda b, i: (b, i, 0)),
        ],
        out_shape=[
            jax.ShapeDtypeStruct((B, S, QK_WIDTH), _BF16),
            jax.ShapeDtypeStruct((B, QK_WIDTH, S), _BF16),
            jax.ShapeDtypeStruct((B, S, D), _BF16),
        ],
        compiler_params=pltpu.CompilerParams(
            dimension_semantics=("arbitrary", "arbitrary"),
            vmem_limit_bytes=V7X_VMEM_LIMIT_BYTES),
        name="qkv_rope",
    )(x, g, w_qkv, ca, cm, cp)


def _diff_attn_kernel(lq1_ref, lk1_ref, lq2_ref, lk2_ref, g_ref,
                      q_ref, kt_ref, v_ref, o_ref, *, lambda_init):
    t = ATTN_TILE
    i = pl.program_id(2)
    q = q_ref[...]
    qs = (q[:, :HEAD_DIM], q[:, HEAD_DIM:])

    def step(j, carry, masked):
        ks = pl.multiple_of(j * t, t)
        kt = kt_ref[:, pl.ds(ks, t)]
        v = v_ref[pl.ds(ks, t), :]
        if masked:
            row = lax.broadcasted_iota(jnp.int32, (t, t), 0)
            col = lax.broadcasted_iota(jnp.int32, (t, t), 1)
            keep = col <= row
        out = []
        for c in range(2):
            m, l, acc = carry[c]
            s = _dot(qs[c], kt[c * HEAD_DIM:(c + 1) * HEAD_DIM, :])
            if masked:
                s = jnp.where(keep, s, -1e30)
            m_new = jnp.maximum(m, jnp.max(s, axis=-1, keepdims=True))
            alpha = jnp.exp2(m - m_new)
            p = jnp.exp2(s - m_new)
            l_new = alpha * l + jnp.sum(p, axis=-1, keepdims=True)
            acc_new = alpha * acc + _dot(p.astype(_BF16), v)
            out.append((m_new, l_new, acc_new))
        return tuple(out)

    init = tuple((jnp.full((t, 1), -jnp.inf, _F32), jnp.zeros((t, 1), _F32),
                  jnp.zeros((t, V_DIM), _F32)) for _ in range(2))
    carry = lax.fori_loop(0, i, lambda j, c: step(j, c, False), init)
    (_, l1, a1), (_, l2, a2) = step(i, carry, True)

    lam = (jnp.exp(jnp.sum(lq1_ref[...] * lk1_ref[...]))
           - jnp.exp(jnp.sum(lq2_ref[...] * lk2_ref[...])) + lambda_init)
    o = a1 / l1 - lam * (a2 / l2)
    o = _rms(o, g_ref[...]) * (1.0 - lambda_init)
    o_ref[...] = o.astype(o_ref.dtype)


def _diff_attn_call(q, kt, v, lq1, lk1, lq2, lk2, g, lambda_init):
    B, S, _ = q.shape
    t = ATTN_TILE
    lam_spec = pl.BlockSpec((1, HEAD_DIM), lambda b, h, i: (0, 0))
    return pl.pallas_call(
        functools.partial(_diff_attn_kernel, lambda_init=lambda_init),
        grid=(B, N_HEADS, S // t),
        in_specs=[
            lam_spec, lam_spec, lam_spec, lam_spec,
            pl.BlockSpec((1, V_DIM), lambda b, h, i: (0, 0)),
            pl.BlockSpec((None, t, V_DIM), lambda b, h, i: (b, i, h)),
            pl.BlockSpec((None, V_DIM, S), lambda b, h, i: (b, h, 0)),
            pl.BlockSpec((None, S, V_DIM), lambda b, h, i: (b, 0, h)),
        ],
        out_specs=pl.BlockSpec((None, t, V_DIM), lambda b, h, i: (b, i, h)),
        out_shape=jax.ShapeDtypeStruct((B, S, N_HEADS * V_DIM), _BF16),
        compiler_params=pltpu.CompilerParams(
            dimension_semantics=("arbitrary", "arbitrary", "arbitrary"),
            vmem_limit_bytes=V7X_VMEM_LIMIT_BYTES),
        name="diff_attn",
    )(lq1, lk1, lq2, lk2, g, q, kt, v)


def _mlp_block(x1, g_pre_ref, w_up_ref, w_down_ref, g_post_ref, act_ref):
    h = _rms(x1, g_pre_ref[...]).astype(_BF16)
    for c in range(D_FF // FF_CHUNK):
        lo, hi = c * FF_CHUNK, (c + 1) * FF_CHUNK
        u = jnp.maximum(_dot(h, w_up_ref[:, lo:hi]), 0.0)
        act_ref[:, lo:hi] = (u * u).astype(_BF16)
    y = _dot(act_ref[...], w_down_ref[...])
    return x1 + _rms(y, g_post_ref[...])


def _attn_out_mlp_kernel(o_ref, x_ref, w_o_ref, g_post_ref, g_mpre_ref,
                         w_up_ref, w_down_ref, g_mpost_ref, out_ref, act_ref):
    m = _dot(o_ref[...], w_o_ref[...])
    x1 = x_ref[...] + _rms(m, g_post_ref[...])
    out_ref[...] = _mlp_block(x1, g_mpre_ref, w_up_ref, w_down_ref,
                              g_mpost_ref, act_ref)


def _attn_out_mlp_call(o, x, w_o, g_post, g_mpre, w_up, w_down, g_mpost):
    T, D = x.shape
    tm = TOKEN_TILE
    row = lambda i: (i, 0)
    return pl.pallas_call(
        _attn_out_mlp_kernel,
        grid=(T // tm,),
        in_specs=[
            pl.BlockSpec((tm, D), row),
            pl.BlockSpec((tm, D), row),
            _resident((D, D)),
            _resident((1, D)),
            _resident((1, D)),
            _resident((D, D_FF)),
            _resident((D_FF, D)),
            _resident((1, D)),
        ],
        out_specs=pl.BlockSpec((tm, D), row),
        out_shape=jax.ShapeDtypeStruct((T, D), _F32),
        scratch_shapes=[pltpu.VMEM((tm, D_FF), _BF16)],
        compiler_params=pltpu.CompilerParams(
            dimension_semantics=("arbitrary",),
            vmem_limit_bytes=V7X_VMEM_LIMIT_BYTES),
        name="attn_out_mlp",
    )(o, x, w_o, g_post, g_mpre, w_up, w_down, g_mpost)


def _conv_mlp_kernel(x_ref, g_pre_ref, w_in_ref, cw_ref, w_out_ref, g_post_ref,
                     g_mpre_ref, w_up_ref, w_down_ref, g_mpost_ref,
                     out_ref, act_ref, gated_ref, tail_ref, *, tiles_per_seq):
    D = D_MODEL
    x = x_ref[...]
    h = _rms(x, g_pre_ref[...]).astype(_BF16)
    hh = _dot(h, w_in_ref[...])
    b_gate = hh[:, :D]
    u = hh[:, D:2 * D] * hh[:, 2 * D:]

    @pl.when(pl.program_id(0) % tiles_per_seq == 0)
    def _():
        tail_ref[...] = jnp.zeros_like(tail_ref)

    w0, w1, w2 = cw_ref[0:1, :], cw_ref[1:2, :], cw_ref[2:3, :]
    y = w0 * pltpu.roll(u, 2, 0) + w1 * pltpu.roll(u, 1, 0) + w2 * u
    gated_ref[...] = (b_gate * y).astype(_BF16)

    ns = V7X_SUBLANES
    tail = tail_ref[...]
    u_top = u[:ns, :]
    rid = lax.broadcasted_iota(jnp.int32, (ns, D), 0)
    u1 = jnp.where(rid < 1, pltpu.roll(tail, 1, 0), pltpu.roll(u_top, 1, 0))
    u2 = jnp.where(rid < 2, pltpu.roll(tail, 2, 0), pltpu.roll(u_top, 2, 0))
    y_top = w0 * u2 + w1 * u1 + w2 * u_top
    gated_ref[:ns, :] = (b_gate[:ns, :] * y_top).astype(_BF16)
    tail_ref[...] = u[-ns:, :]

    m = _dot(gated_ref[...], w_out_ref[...])
    x1 = x + _rms(m, g_post_ref[...])
    out_ref[...] = _mlp_block(x1, g_mpre_ref, w_up_ref, w_down_ref,
                              g_mpost_ref, act_ref)


def _conv_mlp_call(x, seq_len, g_pre, w_in, conv_w, w_out, g_post,
                   g_mpre, w_up, w_down, g_mpost):
    T, D = x.shape
    tm = TOKEN_TILE
    row = lambda i: (i, 0)
    return pl.pallas_call(
        functools.partial(_conv_mlp_kernel, tiles_per_seq=seq_len // tm),
        grid=(T // tm,),
        in_specs=[
            pl.BlockSpec((tm, D), row),
            _resident((1, D)),
            _resident((D, 3 * D)),
            _resident((CONV_WIDTH, D)),
            _resident((D, D)),
            _resident((1, D)),
            _resident((1, D)),
            _resident((D, D_FF)),
            _resident((D_FF, D)),
            _resident((1, D)),
        ],
        out_specs=pl.BlockSpec((tm, D), row),
        out_shape=jax.ShapeDtypeStruct((T, D), _F32),
        scratch_shapes=[
            pltpu.VMEM((tm, D_FF), _BF16),
            pltpu.VMEM((tm, D), _BF16),
            pltpu.VMEM((V7X_SUBLANES, D), _F32),
        ],
        compiler_params=pltpu.CompilerParams(
            dimension_semantics=("arbitrary",),
            vmem_limit_bytes=V7X_VMEM_LIMIT_BYTES),
        name="conv_mlp",
    )(x, g_pre, w_in, conv_w, w_out, g_post, g_mpre, w_up, w_down, g_mpost)


def _rope_tables(seq_len):
    half = ROT_DIM // 2
    pos = jnp.arange(seq_len, dtype=_F32)
    inv_freq = ROPE_THETA ** (-jnp.arange(0, ROT_DIM, 2, dtype=_F32) / ROT_DIM)
    ang = pos[:, None] * inv_freq[None, :]
    cos, sin = jnp.cos(ang), jnp.sin(ang)
    ones = jnp.ones((seq_len, HEAD_DIM - ROT_DIM), _F32)
    zeros = jnp.zeros((seq_len, HEAD_DIM - half), _F32)
    zeros_h = jnp.zeros((seq_len, half), _F32)
    zeros_r = jnp.zeros((seq_len, HEAD_DIM - ROT_DIM), _F32)
    ca = jnp.concatenate([cos, cos, ones], axis=-1)
    cm = jnp.concatenate([-sin, zeros], axis=-1)
    cp = jnp.concatenate([zeros_h, sin, zeros_r], axis=-1)
    two = lambda a: jnp.concatenate([a, a], axis=-1)
    return two(ca), two(cm), two(cp)


def kernel(x, attn_w_qkv, attn_w_o, attn_lambda_q1, attn_lambda_k1, attn_lambda_q2,
           attn_lambda_k2, attn_subln_g, conv_w_in, conv_w, conv_w_out, mlp_w_up,
           mlp_w_down, norm_mixer_pre, norm_mixer_post, norm_mlp_pre, norm_mlp_post):
    B, S, D = x.shape
    row = lambda a: a.reshape(1, -1)
    bf = lambda a: a.astype(_BF16)

    ca, cm, cp = _rope_tables(S)
    q, kt, v = _qkv_rope_call(x, row(norm_mixer_pre[0]), bf(attn_w_qkv[0]), ca, cm, cp)
    o = _diff_attn_call(q, kt, v, row(attn_lambda_q1[0]), row(attn_lambda_k1[0]),
                        row(attn_lambda_q2[0]), row(attn_lambda_k2[0]),
                        row(attn_subln_g[0]), _lambda_init(0))
    x2 = _attn_out_mlp_call(
        o.reshape(B * S, D), x.reshape(B * S, D), bf(attn_w_o[0]),
        row(norm_mixer_post[0]), row(norm_mlp_pre[0]), bf(mlp_w_up[0]),
        bf(mlp_w_down[0]), row(norm_mlp_post[0]))

    x3 = _conv_mlp_call(
        x2, S, row(norm_mixer_pre[1]), bf(conv_w_in[0]), conv_w[0], bf(conv_w_out[0]),
        row(norm_mixer_post[1]), row(norm_mlp_pre[1]), bf(mlp_w_up[1]),
        bf(mlp_w_down[1]), row(norm_mlp_post[1]))
    return x3.reshape(B, S, D)
```

```python
import functools
import math

import jax
import jax.numpy as jnp
from jax import lax
from jax.experimental import pallas as pl
from jax.experimental.pallas import tpu as pltpu

D_MODEL = 1024
HEAD_DIM = 64
V_DIM = 2 * HEAD_DIM
N_HEADS = D_MODEL // V_DIM
QK_WIDTH = N_HEADS * 2 * HEAD_DIM
ROT_DIM = HEAD_DIM // 4
ROPE_THETA = 500000.0
CONV_WIDTH = 3
D_FF = 4 * D_MODEL
EPS = 1e-6

V7X_LANES = 128
V7X_SUBLANES = 8
V7X_VMEM_LIMIT_BYTES = 56 * 1024 * 1024

TOKEN_TILE = 512
ATTN_TILE = 512
ATTN_ROW_CHUNK = 32
FF_CHUNK = 1024
LOG2E = 1.4426950408889634

_BF16 = jnp.bfloat16
_F32 = jnp.float32


def _lambda_init(layer_idx):
    return 0.8 - 0.6 * math.exp(-0.3 * layer_idx)


def _rms(x, g):
    return x * lax.rsqrt(jnp.mean(x * x, axis=-1, keepdims=True) + EPS) * g


def _dot(a, b):
    return jnp.dot(a, b, preferred_element_type=_F32)


def _resident(shape):
    return pl.BlockSpec(shape, lambda *_: (0,) * len(shape),
                        pipeline_mode=pl.Buffered(1))


def _qkv_rope_kernel(x_ref, g_ref, w_ref, ca_ref, cm_ref, cp_ref,
                     q_ref, kt_ref, v_ref):
    h = _rms(x_ref[...], g_ref[...]).astype(_BF16)
    qkv = _dot(h, w_ref[...])
    ca, cm, cp = ca_ref[...], cm_ref[...], cp_ref[...]

    def rope(t):
        return (t * ca + pltpu.roll(t, V7X_LANES - ROT_DIM // 2, 1) * cm
                + pltpu.roll(t, ROT_DIM // 2, 1) * cp)

    q_scale = (HEAD_DIM ** -0.5) * LOG2E
    for hd in range(N_HEADS):
        lo, hi = hd * V_DIM, (hd + 1) * V_DIM
        q_ref[:, lo:hi] = (rope(qkv[:, lo:hi]) * q_scale).astype(_BF16)
        kh = rope(qkv[:, QK_WIDTH + lo:QK_WIDTH + hi])
        kt_ref[lo:hi, :] = kh.T.astype(_BF16)
    v_ref[...] = qkv[:, 2 * QK_WIDTH:].astype(_BF16)


def _qkv_rope_call(x, g, w_qkv, ca, cm, cp):
    B, S, D = x.shape
    tm = TOKEN_TILE
    n_w = w_qkv.shape[1]
    return pl.pallas_call(
        _qkv_rope_kernel,
        grid=(B, S // tm),
        in_specs=[
            pl.BlockSpec((None, tm, D), lambda b, i: (b, i, 0)),
            _resident((1, D)),
            _resident((D, n_w)),
            pl.BlockSpec((tm, V_DIM), lambda b, i: (i, 0)),
            pl.BlockSpec((tm, V_DIM), lambda b, i: (i, 0)),
            pl.BlockSpec((tm, V_DIM), lambda b, i: (i, 0)),
        ],
        out_specs=[
            pl.BlockSpec((None, tm, QK_WIDTH), lambda b, i: (b, i, 0)),
            pl.BlockSpec((None, QK_WIDTH, tm), lambda b, i: (b, 0, i)),
            pl.BlockSpec((None, tm, D), lambda b, i: (b, i, 0)),
        ],
        out_shape=[
            jax.ShapeDtypeStruct((B, S, QK_WIDTH), _BF16),
            jax.ShapeDtypeStruct((B, QK_WIDTH, S), _BF16),
            jax.ShapeDtypeStruct((B, S, D), _BF16),
        ],
        compiler_params=pltpu.CompilerParams(
            dimension_semantics=("arbitrary", "arbitrary"),
            vmem_limit_bytes=V7X_VMEM_LIMIT_BYTES),
        name="qkv_rope",
    )(x, g, w_qkv, ca, cm, cp)


def _diff_attn_kernel(lq1_ref, lk1_ref, lq2_ref, lk2_ref, g_ref,
                      q_ref, kt_ref, v_ref, o_ref,
                      s_ref, p_ref, m_ref, l_ref, acc_ref, *, lambda_init):
    t = ATTN_TILE
    rc = ATTN_ROW_CHUNK
    nl = t // V7X_LANES
    i = pl.program_id(2)

    m_ref[...] = jnp.full(m_ref.shape, -jnp.inf, _F32)
    l_ref[...] = jnp.zeros(l_ref.shape, _F32)
    acc_ref[...] = jnp.zeros(acc_ref.shape, _F32)

    def step(j, masked):
        ks = pl.multiple_of(j * t, t)
        for c in range(2):
            qc = q_ref[:, c * HEAD_DIM:(c + 1) * HEAD_DIM]
            kc = kt_ref[c * HEAD_DIM:(c + 1) * HEAD_DIM, pl.ds(ks, t)]
            s_ref[c] = _dot(qc, kc)
        for c in range(2):
            for r in range(t // rc):
                rows = slice(r * rc, (r + 1) * rc)
                blocks = [s_ref[c, rows, b * V7X_LANES:(b + 1) * V7X_LANES]
                          for b in range(nl)]
                if masked:
                    row = lax.broadcasted_iota(jnp.int32, (rc, V7X_LANES), 0) + r * rc
                    col = lax.broadcasted_iota(jnp.int32, (rc, V7X_LANES), 1)
                    blocks = [jnp.where(col + b * V7X_LANES <= row, blocks[b], -1e30)
                              for b in range(nl)]
                bmax = functools.reduce(jnp.maximum, blocks)
                m_old = m_ref[c, rows, :]
                m_new = jnp.maximum(
                    m_old, jnp.broadcast_to(jnp.max(bmax, axis=-1, keepdims=True),
                                            (rc, V7X_LANES)))
                alpha = jnp.exp2(m_old - m_new)
                ps = [jnp.exp2(blk - m_new) for blk in blocks]
                m_ref[c, rows, :] = m_new
                l_ref[c, rows, :] = alpha * l_ref[c, rows, :] + functools.reduce(jnp.add, ps)
                acc_ref[c, rows, :] = alpha * acc_ref[c, rows, :]
                for b in range(nl):
                    p_ref[c, rows, b * V7X_LANES:(b + 1) * V7X_LANES] = ps[b].astype(_BF16)
        v = v_ref[pl.ds(ks, t), :]
        for c in range(2):
            acc_ref[c] += _dot(p_ref[c], v)

    def body(j, carry):
        step(j, False)
        return carry

    lax.fori_loop(0, i, body, 0)
    step(i, True)

    lam = (jnp.exp(jnp.sum(lq1_ref[...] * lk1_ref[...]))
           - jnp.exp(jnp.sum(lq2_ref[...] * lk2_ref[...])) + lambda_init)
    l1 = jnp.sum(l_ref[0], axis=-1, keepdims=True)
    l2 = jnp.sum(l_ref[1], axis=-1, keepdims=True)
    o = acc_ref[0] / l1 - lam * (acc_ref[1] / l2)
    o = _rms(o, g_ref[...]) * (1.0 - lambda_init)
    o_ref[...] = o.astype(o_ref.dtype)


def _diff_attn_call(q, kt, v, lq1, lk1, lq2, lk2, g, lambda_init):
    B, S, _ = q.shape
    t = ATTN_TILE
    lam_spec = pl.BlockSpec((1, HEAD_DIM), lambda b, h, i: (0, 0))
    return pl.pallas_call(
        functools.partial(_diff_attn_kernel, lambda_init=lambda_init),
        grid=(B, N_HEADS, S // t),
        in_specs=[
            lam_spec, lam_spec, lam_spec, lam_spec,
            pl.BlockSpec((1, V_DIM), lambda b, h, i: (0, 0)),
            pl.BlockSpec((None, t, V_DIM), lambda b, h, i: (b, i, h)),
            pl.BlockSpec((None, V_DIM, S), lambda b, h, i: (b, h, 0)),
            pl.BlockSpec((None, S, V_DIM), lambda b, h, i: (b, 0, h)),
        ],
        out_specs=pl.BlockSpec((None, t, V_DIM), lambda b, h, i: (b, i, h)),
        out_shape=jax.ShapeDtypeStruct((B, S, N_HEADS * V_DIM), _BF16),
        scratch_shapes=[
            pltpu.VMEM((2, t, t), _F32),
            pltpu.VMEM((2, t, t), _BF16),
            pltpu.VMEM((2, t, V7X_LANES), _F32),
            pltpu.VMEM((2, t, V7X_LANES), _F32),
            pltpu.VMEM((2, t, V_DIM), _F32),
        ],
        compiler_params=pltpu.CompilerParams(
            dimension_semantics=("arbitrary", "arbitrary", "arbitrary"),
            vmem_limit_bytes=V7X_VMEM_LIMIT_BYTES),
        name="diff_attn",
    )(lq1, lk1, lq2, lk2, g, q, kt, v)


def _mlp_block(x1, g_pre_ref, w_up_ref, w_down_ref, g_post_ref, act_ref):
    h = _rms(x1, g_pre_ref[...]).astype(_BF16)
    for c in range(D_FF // FF_CHUNK):
        lo, hi = c * FF_CHUNK, (c + 1) * FF_CHUNK
        u = jnp.maximum(_dot(h, w_up_ref[:, lo:hi]), 0.0)
        act_ref[:, lo:hi] = (u * u).astype(_BF16)
    y = _dot(act_ref[...], w_down_ref[...])
    return x1 + _rms(y, g_post_ref[...])


def _attn_out_mlp_kernel(o_ref, x_ref, w_o_ref, g_post_ref, g_mpre_ref,
                         w_up_ref, w_down_ref, g_mpost_ref, out_ref, act_ref):
    m = _dot(o_ref[...], w_o_ref[...])
    x1 = x_ref[...] + _rms(m, g_post_ref[...])
    out_ref[...] = _mlp_block(x1, g_mpre_ref, w_up_ref, w_down_ref,
                              g_mpost_ref, act_ref)


def _attn_out_mlp_call(o, x, w_o, g_post, g_mpre, w_up, w_down, g_mpost):
    T, D = x.shape
    tm = TOKEN_TILE
    row = lambda i: (i, 0)
    return pl.pallas_call(
        _attn_out_mlp_kernel,
        grid=(T // tm,),
        in_specs=[
            pl.BlockSpec((tm, D), row),
            pl.BlockSpec((tm, D), row),
            _resident((D, D)),
            _resident((1, D)),
            _resident((1, D)),
            _resident((D, D_FF)),
            _resident((D_FF, D)),
            _resident((1, D)),
        ],
        out_specs=pl.BlockSpec((tm, D), row),
        out_shape=jax.ShapeDtypeStruct((T, D), _F32),
        scratch_shapes=[pltpu.VMEM((tm, D_FF), _BF16)],
        compiler_params=pltpu.CompilerParams(
            dimension_semantics=("arbitrary",),
            vmem_limit_bytes=V7X_VMEM_LIMIT_BYTES),
        name="attn_out_mlp",
    )(o, x, w_o, g_post, g_mpre, w_up, w_down, g_mpost)


def _conv_mlp_kernel(x_ref, g_pre_ref, w_in_ref, cw_ref, w_out_ref, g_post_ref,
                     g_mpre_ref, w_up_ref, w_down_ref, g_mpost_ref,
                     out_ref, act_ref, gated_ref, tail_ref, *, tiles_per_seq):
    D = D_MODEL
    x = x_ref[...]
    h = _rms(x, g_pre_ref[...]).astype(_BF16)
    hh = _dot(h, w_in_ref[...])
    b_gate = hh[:, :D]
    u = hh[:, D:2 * D] * hh[:, 2 * D:]

    @pl.when(pl.program_id(0) % tiles_per_seq == 0)
    def _():
        tail_ref[...] = jnp.zeros_like(tail_ref)

    w0, w1, w2 = cw_ref[0:1, :], cw_ref[1:2, :], cw_ref[2:3, :]
    y = w0 * pltpu.roll(u, 2, 0) + w1 * pltpu.roll(u, 1, 0) + w2 * u
    gated_ref[...] = (b_gate * y).astype(_BF16)

    ns = V7X_SUBLANES
    tail = tail_ref[...]
    u_top = u[:ns, :]
    rid = lax.broadcasted_iota(jnp.int32, (ns, D), 0)
    u1 = jnp.where(rid < 1, pltpu.roll(tail, 1, 0), pltpu.roll(u_top, 1, 0))
    u2 = jnp.where(rid < 2, pltpu.roll(tail, 2, 0), pltpu.roll(u_top, 2, 0))
    y_top = w0 * u2 + w1 * u1 + w2 * u_top
    gated_ref[:ns, :] = (b_gate[:ns, :] * y_top).astype(_BF16)
    tail_ref[...] = u[-ns:, :]

    m = _dot(gated_ref[...], w_out_ref[...])
    x1 = x + _rms(m, g_post_ref[...])
    out_ref[...] = _mlp_block(x1, g_mpre_ref, w_up_ref, w_down_ref,
                              g_mpost_ref, act_ref)


def _conv_mlp_call(x, seq_len, g_pre, w_in, conv_w, w_out, g_post,
                   g_mpre, w_up, w_down, g_mpost):
    T, D = x.shape
    tm = TOKEN_TILE
    row = lambda i: (i, 0)
    return pl.pallas_call(
        functools.partial(_conv_mlp_kernel, tiles_per_seq=seq_len // tm),
        grid=(T // tm,),
        in_specs=[
            pl.BlockSpec((tm, D), row),
            _resident((1, D)),
            _resident((D, 3 * D)),
            _resident((CONV_WIDTH, D)),
            _resident((D, D)),
            _resident((1, D)),
            _resident((1, D)),
            _resident((D, D_FF)),
            _resident((D_FF, D)),
            _resident((1, D)),
        ],
        out_specs=pl.BlockSpec((tm, D), row),
        out_shape=jax.ShapeDtypeStruct((T, D), _F32),
        scratch_shapes=[
            pltpu.VMEM((tm, D_FF), _BF16),
            pltpu.VMEM((tm, D), _BF16),
            pltpu.VMEM((V7X_SUBLANES, D), _F32),
        ],
        compiler_params=pltpu.CompilerParams(
            dimension_semantics=("arbitrary",),
            vmem_limit_bytes=V7X_VMEM_LIMIT_BYTES),
        name="conv_mlp",
    )(x, g_pre, w_in, conv_w, w_out, g_post, g_mpre, w_up, w_down, g_mpost)


def _rope_tables(seq_len):
    half = ROT_DIM // 2
    pos = jnp.arange(seq_len, dtype=_F32)
    inv_freq = ROPE_THETA ** (-jnp.arange(0, ROT_DIM, 2, dtype=_F32) / ROT_DIM)
    ang = pos[:, None] * inv_freq[None, :]
    cos, sin = jnp.cos(ang), jnp.sin(ang)
    ones = jnp.ones((seq_len, HEAD_DIM - ROT_DIM), _F32)
    zeros = jnp.zeros((seq_len, HEAD_DIM - half), _F32)
    zeros_h = jnp.zeros((seq_len, half), _F32)
    zeros_r = jnp.zeros((seq_len, HEAD_DIM - ROT_DIM), _F32)
    ca = jnp.concatenate([cos, cos, ones], axis=-1)
    cm = jnp.concatenate([-sin, zeros], axis=-1)
    cp = jnp.concatenate([zeros_h, sin, zeros_r], axis=-1)
    two = lambda a: jnp.concatenate([a, a], axis=-1)
    return two(ca), two(cm), two(cp)


def kernel(x, attn_w_qkv, attn_w_o, attn_lambda_q1, attn_lambda_k1, attn_lambda_q2,
           attn_lambda_k2, attn_subln_g, conv_w_in, conv_w, conv_w_out, mlp_w_up,
           mlp_w_down, norm_mixer_pre, norm_mixer_post, norm_mlp_pre, norm_mlp_post):
    B, S, D = x.shape
    row = lambda a: a.reshape(1, -1)
    bf = lambda a: a.astype(_BF16)

    ca, cm, cp = _rope_tables(S)
    q, kt, v = _qkv_rope_call(x, row(norm_mixer_pre[0]), bf(attn_w_qkv[0]), ca, cm, cp)
    o = _diff_attn_call(q, kt, v, row(attn_lambda_q1[0]), row(attn_lambda_k1[0]),
                        row(attn_lambda_q2[0]), row(attn_lambda_k2[0]),
                        row(attn_subln_g[0]), _lambda_init(0))
    x2 = _attn_out_mlp_call(
        o.reshape(B * S, D), x.reshape(B * S, D), bf(attn_w_o[0]),
        row(norm_mixer_post[0]), row(norm_mlp_pre[0]), bf(mlp_w_up[0]),
        bf(mlp_w_down[0]), row(norm_mlp_post[0]))

    x3 = _conv_mlp_call(
        x2, S, row(norm_mixer_pre[1]), bf(conv_w_in[0]), conv_w[0], bf(conv_w_out[0]),
        row(norm_mixer_post[1]), row(norm_mlp_pre[1]), bf(mlp_w_up[1]),
        bf(mlp_w_down[1]), row(norm_mlp_post[1]))
    return x3.reshape(B, S, D)
```

```python
import functools
import math

import jax
import jax.numpy as jnp
from jax import lax
from jax.experimental import pallas as pl
from jax.experimental.pallas import tpu as pltpu

D_MODEL = 1024
HEAD_DIM = 64
V_DIM = 2 * HEAD_DIM
N_HEADS = D_MODEL // V_DIM
QK_WIDTH = N_HEADS * 2 * HEAD_DIM
ROT_DIM = HEAD_DIM // 4
ROPE_THETA = 500000.0
CONV_WIDTH = 3
D_FF = 4 * D_MODEL
EPS = 1e-6

V7X_LANES = 128
V7X_SUBLANES = 8
V7X_VMEM_LIMIT_BYTES = 56 * 1024 * 1024

TOKEN_TILE = 512
ATTN_TILE = 512
ATTN_KEY_CHUNK = 32
FF_CHUNK = 1024
LOG2E = 1.4426950408889634

_BF16 = jnp.bfloat16
_F32 = jnp.float32


def _lambda_init(layer_idx):
    return 0.8 - 0.6 * math.exp(-0.3 * layer_idx)


def _rms(x, g):
    return x * lax.rsqrt(jnp.mean(x * x, axis=-1, keepdims=True) + EPS) * g


def _dot(a, b):
    return jnp.dot(a, b, preferred_element_type=_F32)


def _resident(shape):
    return pl.BlockSpec(shape, lambda *_: (0,) * len(shape),
                        pipeline_mode=pl.Buffered(1))


def _qkv_rope_kernel(x_ref, g_ref, w_ref, ca_ref, cm_ref, cp_ref,
                     qt_ref, k_ref, vt_ref):
    h = _rms(x_ref[...], g_ref[...]).astype(_BF16)
    qkv = _dot(h, w_ref[...])
    ca, cm, cp = ca_ref[...], cm_ref[...], cp_ref[...]

    def rope(t):
        return (t * ca + pltpu.roll(t, V7X_LANES - ROT_DIM // 2, 1) * cm
                + pltpu.roll(t, ROT_DIM // 2, 1) * cp)

    q_scale = (HEAD_DIM ** -0.5) * LOG2E
    for hd in range(N_HEADS):
        lo, hi = hd * V_DIM, (hd + 1) * V_DIM
        qt_ref[lo:hi, :] = (rope(qkv[:, lo:hi]) * q_scale).T.astype(_BF16)
        k_ref[:, lo:hi] = rope(qkv[:, QK_WIDTH + lo:QK_WIDTH + hi]).astype(_BF16)
        vt_ref[lo:hi, :] = qkv[:, 2 * QK_WIDTH + lo:2 * QK_WIDTH + hi].T.astype(_BF16)


def _qkv_rope_call(x, g, w_qkv, ca, cm, cp):
    B, S, D = x.shape
    tm = TOKEN_TILE
    n_w = w_qkv.shape[1]
    return pl.pallas_call(
        _qkv_rope_kernel,
        grid=(B, S // tm),
        in_specs=[
            pl.BlockSpec((None, tm, D), lambda b, i: (b, i, 0)),
            _resident((1, D)),
            _resident((D, n_w)),
            pl.BlockSpec((tm, V_DIM), lambda b, i: (i, 0)),
            pl.BlockSpec((tm, V_DIM), lambda b, i: (i, 0)),
            pl.BlockSpec((tm, V_DIM), lambda b, i: (i, 0)),
        ],
        out_specs=[
            pl.BlockSpec((None, QK_WIDTH, tm), lambda b, i: (b, 0, i)),
            pl.BlockSpec((None, tm, QK_WIDTH), lambda b, i: (b, i, 0)),
            pl.BlockSpec((None, D, tm), lambda b, i: (b, 0, i)),
        ],
        out_shape=[
            jax.ShapeDtypeStruct((B, QK_WIDTH, S), _BF16),
            jax.ShapeDtypeStruct((B, S, QK_WIDTH), _BF16),
            jax.ShapeDtypeStruct((B, D, S), _BF16),
        ],
        compiler_params=pltpu.CompilerParams(
            dimension_semantics=("arbitrary", "arbitrary"),
            vmem_limit_bytes=V7X_VMEM_LIMIT_BYTES),
        name="qkv_rope",
    )(x, g, w_qkv, ca, cm, cp)


def _diff_attn_kernel(lq1_ref, lk1_ref, lq2_ref, lk2_ref, g_ref,
                      qt_ref, k_ref, vt_ref, o_ref,
                      qz_ref, s_ref, p_ref, a_ref, m_ref, l_ref, acc_ref, *, lambda_init):
    t = ATTN_TILE
    kc = ATTN_KEY_CHUNK
    ns = V7X_SUBLANES
    i = pl.program_id(2)

    m_ref[...] = jnp.full(m_ref.shape, -jnp.inf, _F32)
    l_ref[...] = jnp.zeros(l_ref.shape, _F32)
    acc_ref[...] = jnp.zeros(acc_ref.shape, _F32)

    qt = qt_ref[...]
    sub = lax.broadcasted_iota(jnp.int32, qt.shape, 0) // HEAD_DIM
    for c in range(2):
        qz_ref[c] = jnp.where(sub == c, qt, jnp.zeros_like(qt))

    def scores(j, slot):
        k = k_ref[pl.ds(pl.multiple_of(j * t, t), t), :]
        for c in range(2):
            s_ref[slot, c] = _dot(k, qz_ref[c])

    def softmax(slot, masked):
        for c in range(2):
            mx = None
            for r in range(t // kc):
                rows = slice(r * kc, (r + 1) * kc)
                s = s_ref[slot, c, rows, :]
                if masked:
                    key = lax.broadcasted_iota(jnp.int32, (kc, t), 0) + r * kc
                    qry = lax.broadcasted_iota(jnp.int32, (kc, t), 1)
                    s = jnp.where(key <= qry, s, -1e30)
                    s_ref[slot, c, rows, :] = s
                part = jnp.max(s.reshape(kc // ns, ns, t), axis=0)
                mx = part if mx is None else jnp.maximum(mx, part)
            m_old = m_ref[c]
            m_new = jnp.maximum(m_old, jnp.max(mx, axis=0, keepdims=True))
            a_ref[slot, c] = jnp.exp2(m_old - m_new)
            m_ref[c] = m_new
            psum = None
            for r in range(t // kc):
                rows = slice(r * kc, (r + 1) * kc)
                p = jnp.exp2(s_ref[slot, c, rows, :] - m_new)
                part = jnp.sum(p.reshape(kc // ns, ns, t), axis=0)
                psum = part if psum is None else psum + part
                p_ref[slot, c, rows, :] = p.astype(_BF16)
            l_ref[c] = a_ref[slot, c] * l_ref[c] + psum

    def accumulate(j, slot):
        vt = vt_ref[:, pl.ds(pl.multiple_of(j * t, t), t)]
        for c in range(2):
            acc_ref[c] = a_ref[slot, c] * acc_ref[c] + _dot(vt, p_ref[slot, c])

    def stage(j, slot, masked=False, first=False, last=False):
        if not last:
            scores(j + 1, 1 - slot)
        if not first:
            accumulate(j - 1, 1 - slot)
        softmax(slot, masked)

    scores(0, 0)

    @pl.when(i == 0)
    def _():
        stage(0, 0, masked=True, first=True, last=True)
        accumulate(0, 0)

    @pl.when(i > 0)
    def _():
        stage(0, 0, first=True)

        def pair(jj, carry):
            j = 2 * jj + 1
            stage(j, 1)
            stage(j + 1, 0)
            return carry

        lax.fori_loop(0, (i - 1) // 2, pair, 0)

        @pl.when(i % 2 == 0)
        def _():
            stage(i - 1, 1)
            stage(i, 0, masked=True, last=True)
            accumulate(i, 0)

        @pl.when(i % 2 == 1)
        def _():
            stage(i, 1, masked=True, last=True)
            accumulate(i, 1)

    lam = (jnp.exp(jnp.sum(lq1_ref[...] * lk1_ref[...]))
           - jnp.exp(jnp.sum(lq2_ref[...] * lk2_ref[...])) + lambda_init)
    l1 = jnp.sum(l_ref[0], axis=0, keepdims=True)
    l2 = jnp.sum(l_ref[1], axis=0, keepdims=True)
    ot = acc_ref[0] / l1 - lam * (acc_ref[1] / l2)
    ot = ot * lax.rsqrt(jnp.mean(ot * ot, axis=0, keepdims=True) + EPS)
    o = ot.T * g_ref[...] * (1.0 - lambda_init)
    o_ref[...] = o.astype(o_ref.dtype)


def _diff_attn_call(qt, k, vt, lq1, lk1, lq2, lk2, g, lambda_init):
    B, S, _ = k.shape
    t = ATTN_TILE
    lam_spec = pl.BlockSpec((1, HEAD_DIM), lambda b, h, i: (0, 0))
    return pl.pallas_call(
        functools.partial(_diff_attn_kernel, lambda_init=lambda_init),
        grid=(B, N_HEADS, S // t),
        in_specs=[
            lam_spec, lam_spec, lam_spec, lam_spec,
            pl.BlockSpec((1, V_DIM), lambda b, h, i: (0, 0)),
            pl.BlockSpec((None, V_DIM, t), lambda b, h, i: (b, h, i)),
            pl.BlockSpec((None, S, V_DIM), lambda b, h, i: (b, 0, h)),
            pl.BlockSpec((None, V_DIM, S), lambda b, h, i: (b, h, 0)),
        ],
        out_specs=pl.BlockSpec((None, t, V_DIM), lambda b, h, i: (b, i, h)),
        out_shape=jax.ShapeDtypeStruct((B, S, N_HEADS * V_DIM), _BF16),
        scratch_shapes=[
            pltpu.VMEM((2, V_DIM, t), _BF16),
            pltpu.VMEM((2, 2, t, t), _F32),
            pltpu.VMEM((2, 2, t, t), _BF16),
            pltpu.VMEM((2, 2, 1, t), _F32),
            pltpu.VMEM((2, 1, t), _F32),
            pltpu.VMEM((2, V7X_SUBLANES, t), _F32),
            pltpu.VMEM((2, V_DIM, t), _F32),
        ],
        compiler_params=pltpu.CompilerParams(
            dimension_semantics=("arbitrary", "arbitrary", "arbitrary"),
            vmem_limit_bytes=V7X_VMEM_LIMIT_BYTES),
        name="diff_attn",
    )(lq1, lk1, lq2, lk2, g, qt, k, vt)


def _mlp_block(x1, g_pre_ref, w_up_ref, w_down_ref, g_post_ref, act_ref):
    h = _rms(x1, g_pre_ref[...]).astype(_BF16)
    for c in range(D_FF // FF_CHUNK):
        lo, hi = c * FF_CHUNK, (c + 1) * FF_CHUNK
        u = jnp.maximum(_dot(h, w_up_ref[:, lo:hi]), 0.0)
        act_ref[:, lo:hi] = (u * u).astype(_BF16)
    y = _dot(act_ref[...], w_down_ref[...])
    return x1 + _rms(y, g_post_ref[...])


def _attn_out_mlp_kernel(o_ref, x_ref, w_o_ref, g_post_ref, g_mpre_ref,
                         w_up_ref, w_down_ref, g_mpost_ref, out_ref, act_ref):
    m = _dot(o_ref[...], w_o_ref[...])
    x1 = x_ref[...] + _rms(m, g_post_ref[...])
    out_ref[...] = _mlp_block(x1, g_mpre_ref, w_up_ref, w_down_ref,
                              g_mpost_ref, act_ref)


def _attn_out_mlp_call(o, x, w_o, g_post, g_mpre, w_up, w_down, g_mpost):
    T, D = x.shape
    tm = TOKEN_TILE
    row = lambda i: (i, 0)
    return pl.pallas_call(
        _attn_out_mlp_kernel,
        grid=(T // tm,),
        in_specs=[
            pl.BlockSpec((tm, D), row),
            pl.BlockSpec((tm, D), row),
            _resident((D, D)),
            _resident((1, D)),
            _resident((1, D)),
            _resident((D, D_FF)),
            _resident((D_FF, D)),
            _resident((1, D)),
        ],
        out_specs=pl.BlockSpec((tm, D), row),
        out_shape=jax.ShapeDtypeStruct((T, D), _F32),
        scratch_shapes=[pltpu.VMEM((tm, D_FF), _BF16)],
        compiler_params=pltpu.CompilerParams(
            dimension_semantics=("arbitrary",),
            vmem_limit_bytes=V7X_VMEM_LIMIT_BYTES),
        name="attn_out_mlp",
    )(o, x, w_o, g_post, g_mpre, w_up, w_down, g_mpost)


def _conv_mlp_kernel(x_ref, g_pre_ref, w_in_ref, cw_ref, w_out_ref, g_post_ref,
                     g_mpre_ref, w_up_ref, w_down_ref, g_mpost_ref,
                     out_ref, act_ref, gated_ref, tail_ref, *, tiles_per_seq):
    D = D_MODEL
    x = x_ref[...]
    h = _rms(x, g_pre_ref[...]).astype(_BF16)
    hh = _dot(h, w_in_ref[...])
    b_gate = hh[:, :D]
    u = hh[:, D:2 * D] * hh[:, 2 * D:]

    @pl.when(pl.program_id(0) % tiles_per_seq == 0)
    def _():
        tail_ref[...] = jnp.zeros_like(tail_ref)

    w0, w1, w2 = cw_ref[0:1, :], cw_ref[1:2, :], cw_ref[2:3, :]
    y = w0 * pltpu.roll(u, 2, 0) + w1 * pltpu.roll(u, 1, 0) + w2 * u
    gated_ref[...] = (b_gate * y).astype(_BF16)

    ns = V7X_SUBLANES
    tail = tail_ref[...]
    u_top = u[:ns, :]
    rid = lax.broadcasted_iota(jnp.int32, (ns, D), 0)
    u1 = jnp.where(rid < 1, pltpu.roll(tail, 1, 0), pltpu.roll(u_top, 1, 0))
    u2 = jnp.where(rid < 2, pltpu.roll(tail, 2, 0), pltpu.roll(u_top, 2, 0))
    y_top = w0 * u2 + w1 * u1 + w2 * u_top
    gated_ref[:ns, :] = (b_gate[:ns, :] * y_top).astype(_BF16)
    tail_ref[...] = u[-ns:, :]

    m = _dot(gated_ref[...], w_out_ref[...])
    x1 = x + _rms(m, g_post_ref[...])
    out_ref[...] = _mlp_block(x1, g_mpre_ref, w_up_ref, w_down_ref,
                              g_mpost_ref, act_ref)


def _conv_mlp_call(x, seq_len, g_pre, w_in, conv_w, w_out, g_post,
                   g_mpre, w_up, w_down, g_mpost):
    T, D = x.shape
    tm = TOKEN_TILE
    row = lambda i: (i, 0)
    return pl.pallas_call(
        functools.partial(_conv_mlp_kernel, tiles_per_seq=seq_len // tm),
        grid=(T // tm,),
        in_specs=[
            pl.BlockSpec((tm, D), row),
            _resident((1, D)),
            _resident((D, 3 * D)),
            _resident((CONV_WIDTH, D)),
            _resident((D, D)),
            _resident((1, D)),
            _resident((1, D)),
            _resident((D, D_FF)),
            _resident((D_FF, D)),
            _resident((1, D)),
        ],
        out_specs=pl.BlockSpec((tm, D), row),
        out_shape=jax.ShapeDtypeStruct((T, D), _F32),
        scratch_shapes=[
            pltpu.VMEM((tm, D_FF), _BF16),
            pltpu.VMEM((tm, D), _BF16),
            pltpu.VMEM((V7X_SUBLANES, D), _F32),
        ],
        compiler_params=pltpu.CompilerParams(
            dimension_semantics=("arbitrary",),
            vmem_limit_bytes=V7X_VMEM_LIMIT_BYTES),
        name="conv_mlp",
    )(x, g_pre, w_in, conv_w, w_out, g_post, g_mpre, w_up, w_down, g_mpost)


def _rope_tables(seq_len):
    half = ROT_DIM // 2
    pos = jnp.arange(seq_len, dtype=_F32)
    inv_freq = ROPE_THETA ** (-jnp.arange(0, ROT_DIM, 2, dtype=_F32) / ROT_DIM)
    ang = pos[:, None] * inv_freq[None, :]
    cos, sin = jnp.cos(ang), jnp.sin(ang)
    ones = jnp.ones((seq_len, HEAD_DIM - ROT_DIM), _F32)
    zeros = jnp.zeros((seq_len, HEAD_DIM - half), _F32)
    zeros_h = jnp.zeros((seq_len, half), _F32)
    zeros_r = jnp.zeros((seq_len, HEAD_DIM - ROT_DIM), _F32)
    ca = jnp.concatenate([cos, cos, ones], axis=-1)
    cm = jnp.concatenate([-sin, zeros], axis=-1)
    cp = jnp.concatenate([zeros_h, sin, zeros_r], axis=-1)
    two = lambda a: jnp.concatenate([a, a], axis=-1)
    return two(ca), two(cm), two(cp)


def kernel(x, attn_w_qkv, attn_w_o, attn_lambda_q1, attn_lambda_k1, attn_lambda_q2,
           attn_lambda_k2, attn_subln_g, conv_w_in, conv_w, conv_w_out, mlp_w_up,
           mlp_w_down, norm_mixer_pre, norm_mixer_post, norm_mlp_pre, norm_mlp_post):
    B, S, D = x.shape
    row = lambda a: a.reshape(1, -1)
    bf = lambda a: a.astype(_BF16)

    ca, cm, cp = _rope_tables(S)
    qt, k, vt = _qkv_rope_call(x, row(norm_mixer_pre[0]), bf(attn_w_qkv[0]), ca, cm, cp)
    o = _diff_attn_call(qt, k, vt, row(attn_lambda_q1[0]), row(attn_lambda_k1[0]),
                        row(attn_lambda_q2[0]), row(attn_lambda_k2[0]),
                        row(attn_subln_g[0]), _lambda_init(0))
    x2 = _attn_out_mlp_call(
        o.reshape(B * S, D), x.reshape(B * S, D), bf(attn_w_o[0]),
        row(norm_mixer_post[0]), row(norm_mlp_pre[0]), bf(mlp_w_up[0]),
        bf(mlp_w_down[0]), row(norm_mlp_post[0]))

    x3 = _conv_mlp_call(
        x2, S, row(norm_mixer_pre[1]), bf(conv_w_in[0]), conv_w[0], bf(conv_w_out[0]),
        row(norm_mixer_post[1]), row(norm_mlp_pre[1]), bf(mlp_w_up[1]),
        bf(mlp_w_down[1]), row(norm_mlp_post[1]))
    return x3.reshape(B, S, D)
```

```python
import functools
import math

import jax
import jax.numpy as jnp
from jax import lax
from jax.experimental import pallas as pl
from jax.experimental.pallas import tpu as pltpu

D_MODEL = 1024
HEAD_DIM = 64
V_DIM = 2 * HEAD_DIM
N_HEADS = D_MODEL // V_DIM
QK_WIDTH = N_HEADS * 2 * HEAD_DIM
ROT_DIM = HEAD_DIM // 4
ROPE_THETA = 500000.0
CONV_WIDTH = 3
D_FF = 4 * D_MODEL
EPS = 1e-6

V7X_LANES = 128
V7X_SUBLANES = 8
V7X_VMEM_LIMIT_BYTES = 56 * 1024 * 1024

TOKEN_TILE = 512
ATTN_K_TILE = 512
ATTN_Q_TILE = 1024
ATTN_KEY_CHUNK = 16
FF_CHUNK = 1024
LOG2E = 1.4426950408889634

_BF16 = jnp.bfloat16
_F32 = jnp.float32


def _lambda_init(layer_idx):
    return 0.8 - 0.6 * math.exp(-0.3 * layer_idx)


def _rms(x, g):
    return x * lax.rsqrt(jnp.mean(x * x, axis=-1, keepdims=True) + EPS) * g


def _dot(a, b):
    return jnp.dot(a, b, preferred_element_type=_F32)


def _resident(shape):
    return pl.BlockSpec(shape, lambda *_: (0,) * len(shape),
                        pipeline_mode=pl.Buffered(1))


def _qkv_rope_kernel(x_ref, g_ref, w_ref, ca_ref, cm_ref, cp_ref,
                     qt_ref, k_ref, vt_ref):
    h = _rms(x_ref[...], g_ref[...]).astype(_BF16)
    qkv = _dot(h, w_ref[...])
    ca, cm, cp = ca_ref[...], cm_ref[...], cp_ref[...]

    def rope(t):
        return (t * ca + pltpu.roll(t, V7X_LANES - ROT_DIM // 2, 1) * cm
                + pltpu.roll(t, ROT_DIM // 2, 1) * cp)

    q_scale = (HEAD_DIM ** -0.5) * LOG2E
    for hd in range(N_HEADS):
        lo, hi = hd * V_DIM, (hd + 1) * V_DIM
        qt_ref[lo:hi, :] = (rope(qkv[:, lo:hi]) * q_scale).T.astype(_BF16)
        k_ref[:, lo:hi] = rope(qkv[:, QK_WIDTH + lo:QK_WIDTH + hi]).astype(_BF16)
        vt_ref[lo:hi, :] = qkv[:, 2 * QK_WIDTH + lo:2 * QK_WIDTH + hi].T.astype(_BF16)


def _qkv_rope_call(x, g, w_qkv, ca, cm, cp):
    B, S, D = x.shape
    tm = TOKEN_TILE
    n_w = w_qkv.shape[1]
    return pl.pallas_call(
        _qkv_rope_kernel,
        grid=(B, S // tm),
        in_specs=[
            pl.BlockSpec((None, tm, D), lambda b, i: (b, i, 0)),
            _resident((1, D)),
            _resident((D, n_w)),
            pl.BlockSpec((tm, V_DIM), lambda b, i: (i, 0)),
            pl.BlockSpec((tm, V_DIM), lambda b, i: (i, 0)),
            pl.BlockSpec((tm, V_DIM), lambda b, i: (i, 0)),
        ],
        out_specs=[
            pl.BlockSpec((None, QK_WIDTH, tm), lambda b, i: (b, 0, i)),
            pl.BlockSpec((None, tm, QK_WIDTH), lambda b, i: (b, i, 0)),
            pl.BlockSpec((None, D, tm), lambda b, i: (b, 0, i)),
        ],
        out_shape=[
            jax.ShapeDtypeStruct((B, QK_WIDTH, S), _BF16),
            jax.ShapeDtypeStruct((B, S, QK_WIDTH), _BF16),
            jax.ShapeDtypeStruct((B, D, S), _BF16),
        ],
        compiler_params=pltpu.CompilerParams(
            dimension_semantics=("arbitrary", "arbitrary"),
            vmem_limit_bytes=V7X_VMEM_LIMIT_BYTES),
        name="qkv_rope",
    )(x, g, w_qkv, ca, cm, cp)


def _diff_attn_kernel(lq1_ref, lk1_ref, lq2_ref, lk2_ref, g_ref,
                      qt_ref, k_ref, vt_ref, o_ref,
                      qz_ref, s_ref, p_ref, a_ref, m_ref, l_ref, acc_ref, *, lambda_init):
    tq, tk = ATTN_Q_TILE, ATTN_K_TILE
    kc = ATTN_KEY_CHUNK
    ns = V7X_SUBLANES
    i = pl.program_id(2)
    all_q = slice(0, tq)
    top_q = slice(tk, tq)

    m_ref[...] = jnp.full(m_ref.shape, -jnp.inf, _F32)
    l_ref[...] = jnp.zeros(l_ref.shape, _F32)
    acc_ref[...] = jnp.zeros(acc_ref.shape, _F32)

    qt = qt_ref[...]
    sub = lax.broadcasted_iota(jnp.int32, qt.shape, 0) // HEAD_DIM
    for c in range(2):
        qz_ref[c] = jnp.where(sub == c, qt, jnp.zeros_like(qt))

    def scores(j, slot, qs=all_q):
        k = k_ref[pl.ds(pl.multiple_of(j * tk, tk), tk), :]
        for c in range(2):
            s_ref[slot, c, :, qs] = _dot(k, qz_ref[c, :, qs])

    def softmax(slot, qs=all_q, key_off=None):
        nq = qs.stop - qs.start
        for c in range(2):
            mx = None
            for r in range(tk // kc):
                rows = slice(r * kc, (r + 1) * kc)
                s = s_ref[slot, c, rows, qs]
                if key_off is not None:
                    key = lax.broadcasted_iota(jnp.int32, (kc, nq), 0) + (key_off + r * kc)
                    qry = lax.broadcasted_iota(jnp.int32, (kc, nq), 1) + qs.start
                    s = jnp.where(key <= qry, s, -1e30)
                    s_ref[slot, c, rows, qs] = s
                part = jnp.max(s.reshape(kc // ns, ns, nq), axis=0)
                mx = part if mx is None else jnp.maximum(mx, part)
            m_old = m_ref[c, :, qs]
            m_new = jnp.maximum(m_old, jnp.max(mx, axis=0, keepdims=True))
            a_ref[slot, c, :, qs] = jnp.exp2(m_old - m_new)
            m_ref[c, :, qs] = m_new
            psum = None
            for r in range(tk // kc):
                rows = slice(r * kc, (r + 1) * kc)
                p = jnp.exp2(s_ref[slot, c, rows, qs] - m_new)
                part = jnp.sum(p.reshape(kc // ns, ns, nq), axis=0)
                psum = part if psum is None else psum + part
                p_ref[slot, c, rows, qs] = p.astype(_BF16)
            l_ref[c, :, qs] = a_ref[slot, c, :, qs] * l_ref[c, :, qs] + psum

    def accumulate(j, slot, qs=all_q):
        vt = vt_ref[:, pl.ds(pl.multiple_of(j * tk, tk), tk)]
        for c in range(2):
            acc_ref[c, :, qs] = (a_ref[slot, c, :, qs] * acc_ref[c, :, qs]
                                 + _dot(vt, p_ref[slot, c, :, qs]))

    last = 2 * i + 1
    scores(0, 0)

    @pl.when(i == 0)
    def _():
        scores(1, 1, top_q)
        softmax(0, key_off=0)
        accumulate(0, 0)
        softmax(1, top_q, key_off=tk)
        accumulate(1, 1, top_q)

    @pl.when(i > 0)
    def _():
        scores(1, 1)
        softmax(0)

        def pair(jj, carry):
            j = 2 * jj + 1
            scores(j + 1, 0)
            accumulate(j - 1, 0)
            softmax(1)
            scores(j + 2, 1)
            accumulate(j, 1)
            softmax(0)
            return carry

        lax.fori_loop(0, i - 1, pair, 0)

        scores(last - 1, 0)
        accumulate(last - 3, 0)
        softmax(1)
        scores(last, 1, top_q)
        accumulate(last - 2, 1)
        softmax(0, key_off=0)
        accumulate(last - 1, 0)
        softmax(1, top_q, key_off=tk)
        accumulate(last, 1, top_q)

    lam = (jnp.exp(jnp.sum(lq1_ref[...] * lk1_ref[...]))
           - jnp.exp(jnp.sum(lq2_ref[...] * lk2_ref[...])) + lambda_init)
    l1 = jnp.sum(l_ref[0], axis=0, keepdims=True)
    l2 = jnp.sum(l_ref[1], axis=0, keepdims=True)
    ot = acc_ref[0] / l1 - lam * (acc_ref[1] / l2)
    ot = ot * lax.rsqrt(jnp.mean(ot * ot, axis=0, keepdims=True) + EPS)
    o = ot.T * g_ref[...] * (1.0 - lambda_init)
    o_ref[...] = o.astype(o_ref.dtype)


def _diff_attn_call(qt, k, vt, lq1, lk1, lq2, lk2, g, lambda_init):
    B, S, _ = k.shape
    tq, tk = ATTN_Q_TILE, ATTN_K_TILE
    lam_spec = pl.BlockSpec((1, HEAD_DIM), lambda b, h, i: (0, 0))
    return pl.pallas_call(
        functools.partial(_diff_attn_kernel, lambda_init=lambda_init),
        grid=(B, N_HEADS, S // tq),
        in_specs=[
            lam_spec, lam_spec, lam_spec, lam_spec,
            pl.BlockSpec((1, V_DIM), lambda b, h, i: (0, 0)),
            pl.BlockSpec((None, V_DIM, tq), lambda b, h, i: (b, h, i)),
            pl.BlockSpec((None, S, V_DIM), lambda b, h, i: (b, 0, h)),
            pl.BlockSpec((None, V_DIM, S), lambda b, h, i: (b, h, 0)),
        ],
        out_specs=pl.BlockSpec((None, tq, V_DIM), lambda b, h, i: (b, i, h)),
        out_shape=jax.ShapeDtypeStruct((B, S, N_HEADS * V_DIM), _BF16),
        scratch_shapes=[
            pltpu.VMEM((2, V_DIM, tq), _BF16),
            pltpu.VMEM((2, 2, tk, tq), _F32),
            pltpu.VMEM((2, 2, tk, tq), _BF16),
            pltpu.VMEM((2, 2, 1, tq), _F32),
            pltpu.VMEM((2, 1, tq), _F32),
            pltpu.VMEM((2, V7X_SUBLANES, tq), _F32),
            pltpu.VMEM((2, V_DIM, tq), _F32),
        ],
        compiler_params=pltpu.CompilerParams(
            dimension_semantics=("arbitrary", "arbitrary", "arbitrary"),
            vmem_limit_bytes=V7X_VMEM_LIMIT_BYTES),
        name="diff_attn",
    )(lq1, lk1, lq2, lk2, g, qt, k, vt)


def _mlp_block(x1, g_pre_ref, w_up_ref, w_down_ref, g_post_ref, act_ref):
    h = _rms(x1, g_pre_ref[...]).astype(_BF16)
    for c in range(D_FF // FF_CHUNK):
        lo, hi = c * FF_CHUNK, (c + 1) * FF_CHUNK
        u = jnp.maximum(_dot(h, w_up_ref[:, lo:hi]), 0.0)
        act_ref[:, lo:hi] = (u * u).astype(_BF16)
    y = _dot(act_ref[...], w_down_ref[...])
    return x1 + _rms(y, g_post_ref[...])


def _attn_out_mlp_kernel(o_ref, x_ref, w_o_ref, g_post_ref, g_mpre_ref,
                         w_up_ref, w_down_ref, g_mpost_ref, out_ref, act_ref):
    m = _dot(o_ref[...], w_o_ref[...])
    x1 = x_ref[...] + _rms(m, g_post_ref[...])
    out_ref[...] = _mlp_block(x1, g_mpre_ref, w_up_ref, w_down_ref,
                              g_mpost_ref, act_ref)


def _attn_out_mlp_call(o, x, w_o, g_post, g_mpre, w_up, w_down, g_mpost):
    T, D = x.shape
    tm = TOKEN_TILE
    row = lambda i: (i, 0)
    return pl.pallas_call(
        _attn_out_mlp_kernel,
        grid=(T // tm,),
        in_specs=[
            pl.BlockSpec((tm, D), row),
            pl.BlockSpec((tm, D), row),
            _resident((D, D)),
            _resident((1, D)),
            _resident((1, D)),
            _resident((D, D_FF)),
            _resident((D_FF, D)),
            _resident((1, D)),
        ],
        out_specs=pl.BlockSpec((tm, D), row),
        out_shape=jax.ShapeDtypeStruct((T, D), _F32),
        scratch_shapes=[pltpu.VMEM((tm, D_FF), _BF16)],
        compiler_params=pltpu.CompilerParams(
            dimension_semantics=("arbitrary",),
            vmem_limit_bytes=V7X_VMEM_LIMIT_BYTES),
        name="attn_out_mlp",
    )(o, x, w_o, g_post, g_mpre, w_up, w_down, g_mpost)


def _conv_mlp_kernel(x_ref, g_pre_ref, w_in_ref, cw_ref, w_out_ref, g_post_ref,
                     g_mpre_ref, w_up_ref, w_down_ref, g_mpost_ref,
                     out_ref, act_ref, gated_ref, tail_ref, *, tiles_per_seq):
    D = D_MODEL
    x = x_ref[...]
    h = _rms(x, g_pre_ref[...]).astype(_BF16)
    hh = _dot(h, w_in_ref[...])
    b_gate = hh[:, :D]
    u = hh[:, D:2 * D] * hh[:, 2 * D:]

    @pl.when(pl.program_id(0) % tiles_per_seq == 0)
    def _():
        tail_ref[...] = jnp.zeros_like(tail_ref)

    w0, w1, w2 = cw_ref[0:1, :], cw_ref[1:2, :], cw_ref[2:3, :]
    y = w0 * pltpu.roll(u, 2, 0) + w1 * pltpu.roll(u, 1, 0) + w2 * u
    gated_ref[...] = (b_gate * y).astype(_BF16)

    ns = V7X_SUBLANES
    tail = tail_ref[...]
    u_top = u[:ns, :]
    rid = lax.broadcasted_iota(jnp.int32, (ns, D), 0)
    u1 = jnp.where(rid < 1, pltpu.roll(tail, 1, 0), pltpu.roll(u_top, 1, 0))
    u2 = jnp.where(rid < 2, pltpu.roll(tail, 2, 0), pltpu.roll(u_top, 2, 0))
    y_top = w0 * u2 + w1 * u1 + w2 * u_top
    gated_ref[:ns, :] = (b_gate[:ns, :] * y_top).astype(_BF16)
    tail_ref[...] = u[-ns:, :]

    m = _dot(gated_ref[...], w_out_ref[...])
    x1 = x + _rms(m, g_post_ref[...])
    out_ref[...] = _mlp_block(x1, g_mpre_ref, w_up_ref, w_down_ref,
                              g_mpost_ref, act_ref)


def _conv_mlp_call(x, seq_len, g_pre, w_in, conv_w, w_out, g_post,
                   g_mpre, w_up, w_down, g_mpost):
    T, D = x.shape
    tm = TOKEN_TILE
    row = lambda i: (i, 0)
    return pl.pallas_call(
        functools.partial(_conv_mlp_kernel, tiles_per_seq=seq_len // tm),
        grid=(T // tm,),
        in_specs=[
            pl.BlockSpec((tm, D), row),
            _resident((1, D)),
            _resident((D, 3 * D)),
            _resident((CONV_WIDTH, D)),
            _resident((D, D)),
            _resident((1, D)),
            _resident((1, D)),
            _resident((D, D_FF)),
            _resident((D_FF, D)),
            _resident((1, D)),
        ],
        out_specs=pl.BlockSpec((tm, D), row),
        out_shape=jax.ShapeDtypeStruct((T, D), _F32),
        scratch_shapes=[
            pltpu.VMEM((tm, D_FF), _BF16),
            pltpu.VMEM((tm, D), _BF16),
            pltpu.VMEM((V7X_SUBLANES, D), _F32),
        ],
        compiler_params=pltpu.CompilerParams(
            dimension_semantics=("arbitrary",),
            vmem_limit_bytes=V7X_VMEM_LIMIT_BYTES),
        name="conv_mlp",
    )(x, g_pre, w_in, conv_w, w_out, g_post, g_mpre, w_up, w_down, g_mpost)


def _rope_tables(seq_len):
    half = ROT_DIM // 2
    pos = jnp.arange(seq_len, dtype=_F32)
    inv_freq = ROPE_THETA ** (-jnp.arange(0, ROT_DIM, 2, dtype=_F32) / ROT_DIM)
    ang = pos[:, None] * inv_freq[None, :]
    cos, sin = jnp.cos(ang), jnp.sin(ang)
    ones = jnp.ones((seq_len, HEAD_DIM - ROT_DIM), _F32)
    zeros = jnp.zeros((seq_len, HEAD_DIM - half), _F32)
    zeros_h = jnp.zeros((seq_len, half), _F32)
    zeros_r = jnp.zeros((seq_len, HEAD_DIM - ROT_DIM), _F32)
    ca = jnp.concatenate([cos, cos, ones], axis=-1)
    cm = jnp.concatenate([-sin, zeros], axis=-1)
    cp = jnp.concatenate([zeros_h, sin, zeros_r], axis=-1)
    two = lambda a: jnp.concatenate([a, a], axis=-1)
    return two(ca), two(cm), two(cp)


def kernel(x, attn_w_qkv, attn_w_o, attn_lambda_q1, attn_lambda_k1, attn_lambda_q2,
           attn_lambda_k2, attn_subln_g, conv_w_in, conv_w, conv_w_out, mlp_w_up,
           mlp_w_down, norm_mixer_pre, norm_mixer_post, norm_mlp_pre, norm_mlp_post):
    B, S, D = x.shape
    row = lambda a: a.reshape(1, -1)
    bf = lambda a: a.astype(_BF16)

    ca, cm, cp = _rope_tables(S)
    qt, k, vt = _qkv_rope_call(x, row(norm_mixer_pre[0]), bf(attn_w_qkv[0]), ca, cm, cp)
    o = _diff_attn_call(qt, k, vt, row(attn_lambda_q1[0]), row(attn_lambda_k1[0]),
                        row(attn_lambda_q2[0]), row(attn_lambda_k2[0]),
                        row(attn_subln_g[0]), _lambda_init(0))
    x2 = _attn_out_mlp_call(
        o.reshape(B * S, D), x.reshape(B * S, D), bf(attn_w_o[0]),
        row(norm_mixer_post[0]), row(norm_mlp_pre[0]), bf(mlp_w_up[0]),
        bf(mlp_w_down[0]), row(norm_mlp_post[0]))

    x3 = _conv_mlp_call(
        x2, S, row(norm_mixer_pre[1]), bf(conv_w_in[0]), conv_w[0], bf(conv_w_out[0]),
        row(norm_mixer_post[1]), row(norm_mlp_pre[1]), bf(mlp_w_up[1]),
        bf(mlp_w_down[1]), row(norm_mlp_post[1]))
    return x3.reshape(B, S, D)
```

```python
import functools
import math

import jax
import jax.numpy as jnp
import numpy as np
from jax import lax
from jax.experimental import pallas as pl
from jax.experimental.pallas import tpu as pltpu

D_MODEL = 1024
HEAD_DIM = 64
V_DIM = 2 * HEAD_DIM
N_HEADS = D_MODEL // V_DIM
QK_WIDTH = N_HEADS * 2 * HEAD_DIM
ROT_DIM = HEAD_DIM // 4
ROPE_THETA = 500000.0
CONV_WIDTH = 3
D_FF = 4 * D_MODEL
EPS = 1e-6

V7X_LANES = 128
V7X_SUBLANES = 8
V7X_VMEM_LIMIT_BYTES = 56 * 1024 * 1024

TOKEN_TILE = 512
ATTN_K_TILE = 512
ATTN_Q_TILE = 1024
ATTN_KEY_CHUNK = 16
SUM_ROWS = 16
FF_CHUNK = 1024
LOG2E = 1.4426950408889634

_BF16 = jnp.bfloat16
_F32 = jnp.float32


def _lambda_init(layer_idx):
    return 0.8 - 0.6 * math.exp(-0.3 * layer_idx)


def _rms(x, g):
    return x * lax.rsqrt(jnp.mean(x * x, axis=-1, keepdims=True) + EPS) * g


def _dot(a, b):
    return jnp.dot(a, b, preferred_element_type=_F32)


def _resident(shape):
    return pl.BlockSpec(shape, lambda *_: (0,) * len(shape),
                        pipeline_mode=pl.Buffered(1))


def _qkv_rope_kernel(x_ref, g_ref, w_ref, ca_ref, cm_ref, cp_ref,
                     qt_ref, k_ref, vt_ref):
    h = _rms(x_ref[...], g_ref[...]).astype(_BF16)
    qkv = _dot(h, w_ref[...])
    ca, cm, cp = ca_ref[...], cm_ref[...], cp_ref[...]

    def rope(t):
        return (t * ca + pltpu.roll(t, V7X_LANES - ROT_DIM // 2, 1) * cm
                + pltpu.roll(t, ROT_DIM // 2, 1) * cp)

    q_scale = (HEAD_DIM ** -0.5) * LOG2E
    for hd in range(N_HEADS):
        lo, hi = hd * V_DIM, (hd + 1) * V_DIM
        qt_ref[lo:hi, :] = (rope(qkv[:, lo:hi]) * q_scale).T.astype(_BF16)
        k_ref[:, lo:hi] = rope(qkv[:, QK_WIDTH + lo:QK_WIDTH + hi]).astype(_BF16)
        vt_ref[lo:hi, :] = qkv[:, 2 * QK_WIDTH + lo:2 * QK_WIDTH + hi].T.astype(_BF16)


def _qkv_rope_call(x, g, w_qkv, ca, cm, cp):
    B, S, D = x.shape
    tm = TOKEN_TILE
    n_w = w_qkv.shape[1]
    return pl.pallas_call(
        _qkv_rope_kernel,
        grid=(B, S // tm),
        in_specs=[
            pl.BlockSpec((None, tm, D), lambda b, i: (b, i, 0)),
            _resident((1, D)),
            _resident((D, n_w)),
            pl.BlockSpec((tm, V_DIM), lambda b, i: (i, 0)),
            pl.BlockSpec((tm, V_DIM), lambda b, i: (i, 0)),
            pl.BlockSpec((tm, V_DIM), lambda b, i: (i, 0)),
        ],
        out_specs=[
            pl.BlockSpec((None, QK_WIDTH, tm), lambda b, i: (b, 0, i)),
            pl.BlockSpec((None, tm, QK_WIDTH), lambda b, i: (b, i, 0)),
            pl.BlockSpec((None, D, tm), lambda b, i: (b, 0, i)),
        ],
        out_shape=[
            jax.ShapeDtypeStruct((B, QK_WIDTH, S), _BF16),
            jax.ShapeDtypeStruct((B, S, QK_WIDTH), _BF16),
            jax.ShapeDtypeStruct((B, D, S), _BF16),
        ],
        compiler_params=pltpu.CompilerParams(
            dimension_semantics=("arbitrary", "arbitrary"),
            vmem_limit_bytes=V7X_VMEM_LIMIT_BYTES),
        name="qkv_rope",
    )(x, g, w_qkv, ca, cm, cp)


def _diff_attn_kernel(lq1_ref, lk1_ref, lq2_ref, lk2_ref, g_ref,
                      qt_ref, k_ref, vt_ref, o_ref,
                      qz_ref, s_ref, p_ref, a_ref, m_ref, acc_ref, *, lambda_init):
    tq, tk = ATTN_Q_TILE, ATTN_K_TILE
    kc = ATTN_KEY_CHUNK
    ns = V7X_SUBLANES
    i = pl.program_id(2)
    all_q = slice(0, tq)
    top_q = slice(tk, tq)

    m_ref[...] = jnp.full(m_ref.shape, -jnp.inf, _F32)
    acc_ref[...] = jnp.zeros(acc_ref.shape, _F32)

    qt = qt_ref[...]
    sub = lax.broadcasted_iota(jnp.int32, qt.shape, 0) // HEAD_DIM
    for c in range(2):
        qz_ref[c] = jnp.where(sub == c, qt, jnp.zeros_like(qt))

    def scores(j, slot, qs=all_q):
        k = k_ref[pl.ds(pl.multiple_of(j * tk, tk), tk), :]
        for c in range(2):
            s_ref[slot, c, :, qs] = _dot(k, qz_ref[c, :, qs])

    def softmax(slot, qs=all_q, key_off=None):
        nq = qs.stop - qs.start
        for c in range(2):
            mx = None
            for r in range(tk // kc):
                rows = slice(r * kc, (r + 1) * kc)
                s = s_ref[slot, c, rows, qs]
                if key_off is not None:
                    key = lax.broadcasted_iota(jnp.int32, (kc, nq), 0) + (key_off + r * kc)
                    qry = lax.broadcasted_iota(jnp.int32, (kc, nq), 1) + qs.start
                    s = jnp.where(key <= qry, s, -1e30)
                    s_ref[slot, c, rows, qs] = s
                part = jnp.max(s.reshape(kc // ns, ns, nq), axis=0)
                mx = part if mx is None else jnp.maximum(mx, part)
            m_old = m_ref[c, :, qs]
            m_new = jnp.maximum(m_old, jnp.max(mx, axis=0, keepdims=True))
            a_ref[slot, c, :, qs] = jnp.exp2(m_old - m_new)
            m_ref[c, :, qs] = m_new
            for r in range(tk // kc):
                rows = slice(r * kc, (r + 1) * kc)
                p = jnp.exp2(s_ref[slot, c, rows, qs] - m_new)
                p_ref[slot, c, rows, qs] = p.astype(_BF16)

    def accumulate(j, slot, qs=all_q):
        vt = vt_ref[:, pl.ds(pl.multiple_of(j * tk, tk), tk)]
        vt1 = jnp.concatenate([vt, jnp.ones((SUM_ROWS, tk), _BF16)], axis=0)
        for c in range(2):
            acc_ref[c, :, qs] = (a_ref[slot, c, :, qs] * acc_ref[c, :, qs]
                                 + _dot(vt1, p_ref[slot, c, :, qs]))

    last = 2 * i + 1
    scores(0, 0)

    @pl.when(i == 0)
    def _():
        scores(1, 1, top_q)
        softmax(0, key_off=0)
        accumulate(0, 0)
        softmax(1, top_q, key_off=tk)
        accumulate(1, 1, top_q)

    @pl.when(i > 0)
    def _():
        scores(1, 1)
        softmax(0)

        def pair(jj, carry):
            j = 2 * jj + 1
            scores(j + 1, 0)
            accumulate(j - 1, 0)
            softmax(1)
            scores(j + 2, 1)
            accumulate(j, 1)
            softmax(0)
            return carry

        lax.fori_loop(0, i - 1, pair, 0)

        scores(last - 1, 0)
        accumulate(last - 3, 0)
        softmax(1)
        scores(last, 1, top_q)
        accumulate(last - 2, 1)
        softmax(0, key_off=0)
        accumulate(last - 1, 0)
        softmax(1, top_q, key_off=tk)
        accumulate(last, 1, top_q)

    lam = (jnp.exp(jnp.sum(lq1_ref[...] * lk1_ref[...]))
           - jnp.exp(jnp.sum(lq2_ref[...] * lk2_ref[...])) + lambda_init)
    l1 = acc_ref[0, V_DIM:V_DIM + 1, :]
    l2 = acc_ref[1, V_DIM:V_DIM + 1, :]
    ot = acc_ref[0, :V_DIM, :] / l1 - lam * (acc_ref[1, :V_DIM, :] / l2)
    ot = ot * lax.rsqrt(jnp.mean(ot * ot, axis=0, keepdims=True) + EPS)
    o = ot.T * g_ref[...] * (1.0 - lambda_init)
    o_ref[...] = o.astype(o_ref.dtype)


def _diff_attn_call(qt, k, vt, lq1, lk1, lq2, lk2, g, lambda_init):
    B, S, _ = k.shape
    tq, tk = ATTN_Q_TILE, ATTN_K_TILE
    lam_spec = pl.BlockSpec((1, HEAD_DIM), lambda b, h, i: (0, 0))
    return pl.pallas_call(
        functools.partial(_diff_attn_kernel, lambda_init=lambda_init),
        grid=(B, N_HEADS, S // tq),
        in_specs=[
            lam_spec, lam_spec, lam_spec, lam_spec,
            pl.BlockSpec((1, V_DIM), lambda b, h, i: (0, 0)),
            pl.BlockSpec((None, V_DIM, tq), lambda b, h, i: (b, h, i)),
            pl.BlockSpec((None, S, V_DIM), lambda b, h, i: (b, 0, h)),
            pl.BlockSpec((None, V_DIM, S), lambda b, h, i: (b, h, 0)),
        ],
        out_specs=pl.BlockSpec((None, tq, V_DIM), lambda b, h, i: (b, i, h)),
        out_shape=jax.ShapeDtypeStruct((B, S, N_HEADS * V_DIM), _BF16),
        scratch_shapes=[
            pltpu.VMEM((2, V_DIM, tq), _BF16),
            pltpu.VMEM((2, 2, tk, tq), _F32),
            pltpu.VMEM((2, 2, tk, tq), _BF16),
            pltpu.VMEM((2, 2, 1, tq), _F32),
            pltpu.VMEM((2, 1, tq), _F32),
            pltpu.VMEM((2, V_DIM + SUM_ROWS, tq), _F32),
        ],
        compiler_params=pltpu.CompilerParams(
            dimension_semantics=("arbitrary", "arbitrary", "arbitrary"),
            vmem_limit_bytes=V7X_VMEM_LIMIT_BYTES),
        name="diff_attn",
    )(lq1, lk1, lq2, lk2, g, qt, k, vt)


def _mlp_block(x1, g_pre_ref, w_up_ref, w_down_ref, g_post_ref, act_ref):
    h = _rms(x1, g_pre_ref[...]).astype(_BF16)
    for c in range(D_FF // FF_CHUNK):
        lo, hi = c * FF_CHUNK, (c + 1) * FF_CHUNK
        u = jnp.maximum(_dot(h, w_up_ref[:, lo:hi]), 0.0)
        act_ref[:, lo:hi] = (u * u).astype(_BF16)
    y = _dot(act_ref[...], w_down_ref[...])
    return x1 + _rms(y, g_post_ref[...])


def _attn_out_mlp_kernel(o_ref, x_ref, w_o_ref, g_post_ref, g_mpre_ref,
                         w_up_ref, w_down_ref, g_mpost_ref, out_ref, act_ref):
    m = _dot(o_ref[...], w_o_ref[...])
    x1 = x_ref[...] + _rms(m, g_post_ref[...])
    out_ref[...] = _mlp_block(x1, g_mpre_ref, w_up_ref, w_down_ref,
                              g_mpost_ref, act_ref)


def _attn_out_mlp_call(o, x, w_o, g_post, g_mpre, w_up, w_down, g_mpost):
    T, D = x.shape
    tm = TOKEN_TILE
    row = lambda i: (i, 0)
    return pl.pallas_call(
        _attn_out_mlp_kernel,
        grid=(T // tm,),
        in_specs=[
            pl.BlockSpec((tm, D), row),
            pl.BlockSpec((tm, D), row),
            _resident((D, D)),
            _resident((1, D)),
            _resident((1, D)),
            _resident((D, D_FF)),
            _resident((D_FF, D)),
            _resident((1, D)),
        ],
        out_specs=pl.BlockSpec((tm, D), row),
        out_shape=jax.ShapeDtypeStruct((T, D), _F32),
        scratch_shapes=[pltpu.VMEM((tm, D_FF), _BF16)],
        compiler_params=pltpu.CompilerParams(
            dimension_semantics=("arbitrary",),
            vmem_limit_bytes=V7X_VMEM_LIMIT_BYTES),
        name="attn_out_mlp",
    )(o, x, w_o, g_post, g_mpre, w_up, w_down, g_mpost)


def _conv_mlp_kernel(x_ref, g_pre_ref, w_in_ref, cw_ref, w_out_ref, g_post_ref,
                     g_mpre_ref, w_up_ref, w_down_ref, g_mpost_ref,
                     out_ref, act_ref, gated_ref, tail_ref, *, tiles_per_seq):
    D = D_MODEL
    x = x_ref[...]
    h = _rms(x, g_pre_ref[...]).astype(_BF16)
    hh = _dot(h, w_in_ref[...])
    b_gate = hh[:, :D]
    u = hh[:, D:2 * D] * hh[:, 2 * D:]

    @pl.when(pl.program_id(0) % tiles_per_seq == 0)
    def _():
        tail_ref[...] = jnp.zeros_like(tail_ref)

    w0, w1, w2 = cw_ref[0:1, :], cw_ref[1:2, :], cw_ref[2:3, :]
    y = w0 * pltpu.roll(u, 2, 0) + w1 * pltpu.roll(u, 1, 0) + w2 * u
    gated_ref[...] = (b_gate * y).astype(_BF16)

    ns = V7X_SUBLANES
    tail = tail_ref[...]
    u_top = u[:ns, :]
    rid = lax.broadcasted_iota(jnp.int32, (ns, D), 0)
    u1 = jnp.where(rid < 1, pltpu.roll(tail, 1, 0), pltpu.roll(u_top, 1, 0))
    u2 = jnp.where(rid < 2, pltpu.roll(tail, 2, 0), pltpu.roll(u_top, 2, 0))
    y_top = w0 * u2 + w1 * u1 + w2 * u_top
    gated_ref[:ns, :] = (b_gate[:ns, :] * y_top).astype(_BF16)
    tail_ref[...] = u[-ns:, :]

    m = _dot(gated_ref[...], w_out_ref[...])
    x1 = x + _rms(m, g_post_ref[...])
    out_ref[...] = _mlp_block(x1, g_mpre_ref, w_up_ref, w_down_ref,
                              g_mpost_ref, act_ref)


def _conv_mlp_call(x, seq_len, g_pre, w_in, conv_w, w_out, g_post,
                   g_mpre, w_up, w_down, g_mpost):
    T, D = x.shape
    tm = TOKEN_TILE
    row = lambda i: (i, 0)
    return pl.pallas_call(
        functools.partial(_conv_mlp_kernel, tiles_per_seq=seq_len // tm),
        grid=(T // tm,),
        in_specs=[
            pl.BlockSpec((tm, D), row),
            _resident((1, D)),
            _resident((D, 3 * D)),
            _resident((CONV_WIDTH, D)),
            _resident((D, D)),
            _resident((1, D)),
            _resident((1, D)),
            _resident((D, D_FF)),
            _resident((D_FF, D)),
            _resident((1, D)),
        ],
        out_specs=pl.BlockSpec((tm, D), row),
        out_shape=jax.ShapeDtypeStruct((T, D), _F32),
        scratch_shapes=[
            pltpu.VMEM((tm, D_FF), _BF16),
            pltpu.VMEM((tm, D), _BF16),
            pltpu.VMEM((V7X_SUBLANES, D), _F32),
        ],
        compiler_params=pltpu.CompilerParams(
            dimension_semantics=("arbitrary",),
            vmem_limit_bytes=V7X_VMEM_LIMIT_BYTES),
        name="conv_mlp",
    )(x, g_pre, w_in, conv_w, w_out, g_post, g_mpre, w_up, w_down, g_mpost)


def _rope_tables(seq_len):
    half = ROT_DIM // 2
    pos = np.arange(seq_len, dtype=np.float64)
    inv_freq = ROPE_THETA ** (-np.arange(0, ROT_DIM, 2, dtype=np.float64) / ROT_DIM)
    ang = pos[:, None] * inv_freq[None, :]
    cos = np.cos(ang).astype(np.float32)
    sin = np.sin(ang).astype(np.float32)
    ones = np.ones((seq_len, HEAD_DIM - ROT_DIM), np.float32)
    zeros = np.zeros((seq_len, HEAD_DIM - half), np.float32)
    zeros_h = np.zeros((seq_len, half), np.float32)
    zeros_r = np.zeros((seq_len, HEAD_DIM - ROT_DIM), np.float32)
    ca = np.concatenate([cos, cos, ones], axis=-1)
    cm = np.concatenate([-sin, zeros], axis=-1)
    cp = np.concatenate([zeros_h, sin, zeros_r], axis=-1)
    two = lambda a: jnp.asarray(np.concatenate([a, a], axis=-1))
    return two(ca), two(cm), two(cp)


def kernel(x, attn_w_qkv, attn_w_o, attn_lambda_q1, attn_lambda_k1, attn_lambda_q2,
           attn_lambda_k2, attn_subln_g, conv_w_in, conv_w, conv_w_out, mlp_w_up,
           mlp_w_down, norm_mixer_pre, norm_mixer_post, norm_mlp_pre, norm_mlp_post):
    B, S, D = x.shape
    row = lambda a: a.reshape(1, -1)
    bf = lambda a: a.astype(_BF16)

    ca, cm, cp = _rope_tables(S)
    qt, k, vt = _qkv_rope_call(x, row(norm_mixer_pre[0]), bf(attn_w_qkv[0]), ca, cm, cp)
    o = _diff_attn_call(qt, k, vt, row(attn_lambda_q1[0]), row(attn_lambda_k1[0]),
                        row(attn_lambda_q2[0]), row(attn_lambda_k2[0]),
                        row(attn_subln_g[0]), _lambda_init(0))
    x2 = _attn_out_mlp_call(
        o.reshape(B * S, D), x.reshape(B * S, D), bf(attn_w_o[0]),
        row(norm_mixer_post[0]), row(norm_mlp_pre[0]), bf(mlp_w_up[0]),
        bf(mlp_w_down[0]), row(norm_mlp_post[0]))

    x3 = _conv_mlp_call(
        x2, S, row(norm_mixer_pre[1]), bf(conv_w_in[0]), conv_w[0], bf(conv_w_out[0]),
        row(norm_mixer_post[1]), row(norm_mlp_pre[1]), bf(mlp_w_up[1]),
        bf(mlp_w_down[1]), row(norm_mlp_post[1]))
    return x3.reshape(B, S, D)
```

```python
import functools
import math

import jax
import jax.numpy as jnp
import numpy as np
from jax import lax
from jax.experimental import pallas as pl
from jax.experimental.pallas import tpu as pltpu

D_MODEL = 1024
HEAD_DIM = 64
V_DIM = 2 * HEAD_DIM
N_HEADS = D_MODEL // V_DIM
QK_WIDTH = N_HEADS * 2 * HEAD_DIM
ROT_DIM = HEAD_DIM // 4
ROPE_THETA = 500000.0
CONV_WIDTH = 3
D_FF = 4 * D_MODEL
EPS = 1e-6

V7X_LANES = 128
V7X_SUBLANES = 8
V7X_VMEM_LIMIT_BYTES = 56 * 1024 * 1024

TOKEN_TILE = 512
ATTN_K_TILE = 512
ATTN_Q_TILE = 1024
ATTN_KEY_CHUNK = 16
SUM_ROWS = 16
FF_CHUNK = 1024
LOG2E = 1.4426950408889634

_BF16 = jnp.bfloat16
_F32 = jnp.float32


def _lambda_init(layer_idx):
    return 0.8 - 0.6 * math.exp(-0.3 * layer_idx)


def _rms(x, g):
    return x * lax.rsqrt(jnp.mean(x * x, axis=-1, keepdims=True) + EPS) * g


def _dot(a, b):
    return jnp.dot(a, b, preferred_element_type=_F32)


def _resident(shape):
    return pl.BlockSpec(shape, lambda *_: (0,) * len(shape),
                        pipeline_mode=pl.Buffered(1))


def _qkv_rope_kernel(x_ref, g_ref, w_ref, ca_ref, cm_ref, cp_ref,
                     qt_ref, k_ref, vt_ref):
    h = _rms(x_ref[...], g_ref[...]).astype(_BF16)
    qkv = _dot(h, w_ref[...])
    ca, cm, cp = ca_ref[...], cm_ref[...], cp_ref[...]

    def rope(t):
        return (t * ca + pltpu.roll(t, V7X_LANES - ROT_DIM // 2, 1) * cm
                + pltpu.roll(t, ROT_DIM // 2, 1) * cp)

    q_scale = (HEAD_DIM ** -0.5) * LOG2E
    for hd in range(N_HEADS):
        lo, hi = hd * V_DIM, (hd + 1) * V_DIM
        qt_ref[lo:hi, :] = (rope(qkv[:, lo:hi]) * q_scale).T.astype(_BF16)
        k_ref[:, lo:hi] = rope(qkv[:, QK_WIDTH + lo:QK_WIDTH + hi]).astype(_BF16)
        vt_ref[lo:hi, :] = qkv[:, 2 * QK_WIDTH + lo:2 * QK_WIDTH + hi].T.astype(_BF16)


def _qkv_rope_call(x, g, w_qkv, ca, cm, cp):
    B, S, D = x.shape
    tm = TOKEN_TILE
    n_w = w_qkv.shape[1]
    return pl.pallas_call(
        _qkv_rope_kernel,
        grid=(B, S // tm),
        in_specs=[
            pl.BlockSpec((None, tm, D), lambda b, i: (b, i, 0)),
            _resident((1, D)),
            _resident((D, n_w)),
            pl.BlockSpec((tm, V_DIM), lambda b, i: (i, 0)),
            pl.BlockSpec((tm, V_DIM), lambda b, i: (i, 0)),
            pl.BlockSpec((tm, V_DIM), lambda b, i: (i, 0)),
        ],
        out_specs=[
            pl.BlockSpec((None, QK_WIDTH, tm), lambda b, i: (b, 0, i)),
            pl.BlockSpec((None, tm, QK_WIDTH), lambda b, i: (b, i, 0)),
            pl.BlockSpec((None, D, tm), lambda b, i: (b, 0, i)),
        ],
        out_shape=[
            jax.ShapeDtypeStruct((B, QK_WIDTH, S), _BF16),
            jax.ShapeDtypeStruct((B, S, QK_WIDTH), _BF16),
            jax.ShapeDtypeStruct((B, D, S), _BF16),
        ],
        compiler_params=pltpu.CompilerParams(
            dimension_semantics=("arbitrary", "arbitrary"),
            vmem_limit_bytes=V7X_VMEM_LIMIT_BYTES),
        name="qkv_rope",
    )(x, g, w_qkv, ca, cm, cp)


def _diff_attn_kernel(lq1_ref, lk1_ref, lq2_ref, lk2_ref, g_ref,
                      qt_ref, k_ref, vt_ref, o_ref,
                      qz_ref, s_ref, x_ref, p_ref, a_ref, m_ref, acc_ref, *, lambda_init):
    tq, tk = ATTN_Q_TILE, ATTN_K_TILE
    kc = ATTN_KEY_CHUNK
    ns = V7X_SUBLANES
    i = pl.program_id(2)
    all_q = slice(0, tq)
    top_q = slice(tk, tq)

    m_ref[...] = jnp.full(m_ref.shape, -jnp.inf, _F32)
    acc_ref[...] = jnp.zeros(acc_ref.shape, _F32)

    qt = qt_ref[...]
    sub = lax.broadcasted_iota(jnp.int32, qt.shape, 0) // HEAD_DIM
    for c in range(2):
        qz_ref[c] = jnp.where(sub == c, qt, jnp.zeros_like(qt))

    def scores(j, slot, qs=all_q, key_off=None):
        nq = qs.stop - qs.start
        k = k_ref[pl.ds(pl.multiple_of(j * tk, tk), tk), :]
        for c in range(2):
            s = _dot(k, qz_ref[c, :, qs])
            if key_off is not None:
                key = lax.broadcasted_iota(jnp.int32, (tk, nq), 0) + key_off
                qry = lax.broadcasted_iota(jnp.int32, (tk, nq), 1) + qs.start
                s = jnp.where(key <= qry, s, -1e30)
            s_ref[slot, c, :, qs] = s
            x_ref[slot, c, :, qs] = jnp.max(s.reshape(tk // ns, ns, nq), axis=0)

    def softmax(slot, qs=all_q):
        for c in range(2):
            m_old = m_ref[c, :, qs]
            m_new = jnp.maximum(m_old, jnp.max(x_ref[slot, c, :, qs], axis=0, keepdims=True))
            a_ref[slot, c, :, qs] = jnp.exp2(m_old - m_new)
            m_ref[c, :, qs] = m_new
            for r in range(tk // kc):
                rows = slice(r * kc, (r + 1) * kc)
                p = jnp.exp2(s_ref[slot, c, rows, qs] - m_new)
                p_ref[slot, c, rows, qs] = p.astype(_BF16)

    def accumulate(j, slot, qs=all_q):
        vt = vt_ref[:, pl.ds(pl.multiple_of(j * tk, tk), tk)]
        vt1 = jnp.concatenate([vt, jnp.ones((SUM_ROWS, tk), _BF16)], axis=0)
        for c in range(2):
            acc_ref[c, :, qs] = (a_ref[slot, c, :, qs] * acc_ref[c, :, qs]
                                 + _dot(vt1, p_ref[slot, c, :, qs]))

    last = 2 * i + 1

    @pl.when(i == 0)
    def _():
        scores(0, 0, key_off=0)
        scores(1, 1, top_q, key_off=tk)
        softmax(0)
        accumulate(0, 0)
        softmax(1, top_q)
        accumulate(1, 1, top_q)

    @pl.when(i > 0)
    def _():
        scores(0, 0)
        scores(1, 1)
        softmax(0)

        def pair(jj, carry):
            j = 2 * jj + 1
            scores(j + 1, 0)
            accumulate(j - 1, 0)
            softmax(1)
            scores(j + 2, 1)
            accumulate(j, 1)
            softmax(0)
            return carry

        lax.fori_loop(0, i - 1, pair, 0)

        scores(last - 1, 0, key_off=0)
        accumulate(last - 3, 0)
        softmax(1)
        scores(last, 1, top_q, key_off=tk)
        accumulate(last - 2, 1)
        softmax(0)
        accumulate(last - 1, 0)
        softmax(1, top_q)
        accumulate(last, 1, top_q)

    lam = (jnp.exp(jnp.sum(lq1_ref[...] * lk1_ref[...]))
           - jnp.exp(jnp.sum(lq2_ref[...] * lk2_ref[...])) + lambda_init)
    l1 = acc_ref[0, V_DIM:V_DIM + 1, :]
    l2 = acc_ref[1, V_DIM:V_DIM + 1, :]
    ot = acc_ref[0, :V_DIM, :] / l1 - lam * (acc_ref[1, :V_DIM, :] / l2)
    ot = ot * lax.rsqrt(jnp.mean(ot * ot, axis=0, keepdims=True) + EPS)
    o = ot.T * g_ref[...] * (1.0 - lambda_init)
    o_ref[...] = o.astype(o_ref.dtype)


def _diff_attn_call(qt, k, vt, lq1, lk1, lq2, lk2, g, lambda_init):
    B, S, _ = k.shape
    tq, tk = ATTN_Q_TILE, ATTN_K_TILE
    lam_spec = pl.BlockSpec((1, HEAD_DIM), lambda b, h, i: (0, 0))
    return pl.pallas_call(
        functools.partial(_diff_attn_kernel, lambda_init=lambda_init),
        grid=(B, N_HEADS, S // tq),
        in_specs=[
            lam_spec, lam_spec, lam_spec, lam_spec,
            pl.BlockSpec((1, V_DIM), lambda b, h, i: (0, 0)),
            pl.BlockSpec((None, V_DIM, tq), lambda b, h, i: (b, h, i)),
            pl.BlockSpec((None, S, V_DIM), lambda b, h, i: (b, 0, h)),
            pl.BlockSpec((None, V_DIM, S), lambda b, h, i: (b, h, 0)),
        ],
        out_specs=pl.BlockSpec((None, tq, V_DIM), lambda b, h, i: (b, i, h)),
        out_shape=jax.ShapeDtypeStruct((B, S, N_HEADS * V_DIM), _BF16),
        scratch_shapes=[
            pltpu.VMEM((2, V_DIM, tq), _BF16),
            pltpu.VMEM((2, 2, tk, tq), _F32),
            pltpu.VMEM((2, 2, V7X_SUBLANES, tq), _F32),
            pltpu.VMEM((2, 2, tk, tq), _BF16),
            pltpu.VMEM((2, 2, 1, tq), _F32),
            pltpu.VMEM((2, 1, tq), _F32),
            pltpu.VMEM((2, V_DIM + SUM_ROWS, tq), _F32),
        ],
        compiler_params=pltpu.CompilerParams(
            dimension_semantics=("arbitrary", "arbitrary", "arbitrary"),
            vmem_limit_bytes=V7X_VMEM_LIMIT_BYTES),
        name="diff_attn",
    )(lq1, lk1, lq2, lk2, g, qt, k, vt)


def _mlp_block(x1, g_pre_ref, w_up_ref, w_down_ref, g_post_ref, act_ref):
    h = _rms(x1, g_pre_ref[...]).astype(_BF16)
    for c in range(D_FF // FF_CHUNK):
        lo, hi = c * FF_CHUNK, (c + 1) * FF_CHUNK
        u = jnp.maximum(_dot(h, w_up_ref[:, lo:hi]), 0.0)
        act_ref[:, lo:hi] = (u * u).astype(_BF16)
    y = _dot(act_ref[...], w_down_ref[...])
    return x1 + _rms(y, g_post_ref[...])


def _attn_out_mlp_kernel(o_ref, x_ref, w_o_ref, g_post_ref, g_mpre_ref,
                         w_up_ref, w_down_ref, g_mpost_ref, out_ref, act_ref):
    m = _dot(o_ref[...], w_o_ref[...])
    x1 = x_ref[...] + _rms(m, g_post_ref[...])
    out_ref[...] = _mlp_block(x1, g_mpre_ref, w_up_ref, w_down_ref,
                              g_mpost_ref, act_ref)


def _attn_out_mlp_call(o, x, w_o, g_post, g_mpre, w_up, w_down, g_mpost):
    T, D = x.shape
    tm = TOKEN_TILE
    row = lambda i: (i, 0)
    return pl.pallas_call(
        _attn_out_mlp_kernel,
        grid=(T // tm,),
        in_specs=[
            pl.BlockSpec((tm, D), row),
            pl.BlockSpec((tm, D), row),
            _resident((D, D)),
            _resident((1, D)),
            _resident((1, D)),
            _resident((D, D_FF)),
            _resident((D_FF, D)),
            _resident((1, D)),
        ],
        out_specs=pl.BlockSpec((tm, D), row),
        out_shape=jax.ShapeDtypeStruct((T, D), _F32),
        scratch_shapes=[pltpu.VMEM((tm, D_FF), _BF16)],
        compiler_params=pltpu.CompilerParams(
            dimension_semantics=("arbitrary",),
            vmem_limit_bytes=V7X_VMEM_LIMIT_BYTES),
        name="attn_out_mlp",
    )(o, x, w_o, g_post, g_mpre, w_up, w_down, g_mpost)


def _conv_mlp_kernel(x_ref, g_pre_ref, w_in_ref, cw_ref, w_out_ref, g_post_ref,
                     g_mpre_ref, w_up_ref, w_down_ref, g_mpost_ref,
                     out_ref, act_ref, gated_ref, tail_ref, *, tiles_per_seq):
    D = D_MODEL
    x = x_ref[...]
    h = _rms(x, g_pre_ref[...]).astype(_BF16)
    hh = _dot(h, w_in_ref[...])
    b_gate = hh[:, :D]
    u = hh[:, D:2 * D] * hh[:, 2 * D:]

    @pl.when(pl.program_id(0) % tiles_per_seq == 0)
    def _():
        tail_ref[...] = jnp.zeros_like(tail_ref)

    w0, w1, w2 = cw_ref[0:1, :], cw_ref[1:2, :], cw_ref[2:3, :]
    y = w0 * pltpu.roll(u, 2, 0) + w1 * pltpu.roll(u, 1, 0) + w2 * u
    gated_ref[...] = (b_gate * y).astype(_BF16)

    ns = V7X_SUBLANES
    tail = tail_ref[...]
    u_top = u[:ns, :]
    rid = lax.broadcasted_iota(jnp.int32, (ns, D), 0)
    u1 = jnp.where(rid < 1, pltpu.roll(tail, 1, 0), pltpu.roll(u_top, 1, 0))
    u2 = jnp.where(rid < 2, pltpu.roll(tail, 2, 0), pltpu.roll(u_top, 2, 0))
    y_top = w0 * u2 + w1 * u1 + w2 * u_top
    gated_ref[:ns, :] = (b_gate[:ns, :] * y_top).astype(_BF16)
    tail_ref[...] = u[-ns:, :]

    m = _dot(gated_ref[...], w_out_ref[...])
    x1 = x + _rms(m, g_post_ref[...])
    out_ref[...] = _mlp_block(x1, g_mpre_ref, w_up_ref, w_down_ref,
                              g_mpost_ref, act_ref)


def _conv_mlp_call(x, seq_len, g_pre, w_in, conv_w, w_out, g_post,
                   g_mpre, w_up, w_down, g_mpost):
    T, D = x.shape
    tm = TOKEN_TILE
    row = lambda i: (i, 0)
    return pl.pallas_call(
        functools.partial(_conv_mlp_kernel, tiles_per_seq=seq_len // tm),
        grid=(T // tm,),
        in_specs=[
            pl.BlockSpec((tm, D), row),
            _resident((1, D)),
            _resident((D, 3 * D)),
            _resident((CONV_WIDTH, D)),
            _resident((D, D)),
            _resident((1, D)),
            _resident((1, D)),
            _resident((D, D_FF)),
            _resident((D_FF, D)),
            _resident((1, D)),
        ],
        out_specs=pl.BlockSpec((tm, D), row),
        out_shape=jax.ShapeDtypeStruct((T, D), _F32),
        scratch_shapes=[
            pltpu.VMEM((tm, D_FF), _BF16),
            pltpu.VMEM((tm, D), _BF16),
            pltpu.VMEM((V7X_SUBLANES, D), _F32),
        ],
        compiler_params=pltpu.CompilerParams(
            dimension_semantics=("arbitrary",),
            vmem_limit_bytes=V7X_VMEM_LIMIT_BYTES),
        name="conv_mlp",
    )(x, g_pre, w_in, conv_w, w_out, g_post, g_mpre, w_up, w_down, g_mpost)


def _rope_tables(seq_len):
    half = ROT_DIM // 2
    pos = np.arange(seq_len, dtype=np.float64)
    inv_freq = ROPE_THETA ** (-np.arange(0, ROT_DIM, 2, dtype=np.float64) / ROT_DIM)
    ang = pos[:, None] * inv_freq[None, :]
    cos = np.cos(ang).astype(np.float32)
    sin = np.sin(ang).astype(np.float32)
    ones = np.ones((seq_len, HEAD_DIM - ROT_DIM), np.float32)
    zeros = np.zeros((seq_len, HEAD_DIM - half), np.float32)
    zeros_h = np.zeros((seq_len, half), np.float32)
    zeros_r = np.zeros((seq_len, HEAD_DIM - ROT_DIM), np.float32)
    ca = np.concatenate([cos, cos, ones], axis=-1)
    cm = np.concatenate([-sin, zeros], axis=-1)
    cp = np.concatenate([zeros_h, sin, zeros_r], axis=-1)
    two = lambda a: jnp.asarray(np.concatenate([a, a], axis=-1))
    return two(ca), two(cm), two(cp)


def kernel(x, attn_w_qkv, attn_w_o, attn_lambda_q1, attn_lambda_k1, attn_lambda_q2,
           attn_lambda_k2, attn_subln_g, conv_w_in, conv_w, conv_w_out, mlp_w_up,
           mlp_w_down, norm_mixer_pre, norm_mixer_post, norm_mlp_pre, norm_mlp_post):
    B, S, D = x.shape
    row = lambda a: a.reshape(1, -1)
    bf = lambda a: a.astype(_BF16)

    ca, cm, cp = _rope_tables(S)
    qt, k, vt = _qkv_rope_call(x, row(norm_mixer_pre[0]), bf(attn_w_qkv[0]), ca, cm, cp)
    o = _diff_attn_call(qt, k, vt, row(attn_lambda_q1[0]), row(attn_lambda_k1[0]),
                        row(attn_lambda_q2[0]), row(attn_lambda_k2[0]),
                        row(attn_subln_g[0]), _lambda_init(0))
    x2 = _attn_out_mlp_call(
        o.reshape(B * S, D), x.reshape(B * S, D), bf(attn_w_o[0]),
        row(norm_mixer_post[0]), row(norm_mlp_pre[0]), bf(mlp_w_up[0]),
        bf(mlp_w_down[0]), row(norm_mlp_post[0]))

    x3 = _conv_mlp_call(
        x2, S, row(norm_mixer_pre[1]), bf(conv_w_in[0]), conv_w[0], bf(conv_w_out[0]),
        row(norm_mixer_post[1]), row(norm_mlp_pre[1]), bf(mlp_w_up[1]),
        bf(mlp_w_down[1]), row(norm_mlp_post[1]))
    return x3.reshape(B, S, D)
```

```python
import functools
import math

import jax
import jax.numpy as jnp
import numpy as np
from jax import lax
from jax.experimental import pallas as pl
from jax.experimental.pallas import tpu as pltpu

D_MODEL = 1024
HEAD_DIM = 64
V_DIM = 2 * HEAD_DIM
N_HEADS = D_MODEL // V_DIM
QK_WIDTH = N_HEADS * 2 * HEAD_DIM
ROT_DIM = HEAD_DIM // 4
ROPE_THETA = 500000.0
CONV_WIDTH = 3
D_FF = 4 * D_MODEL
EPS = 1e-6

V7X_LANES = 128
V7X_SUBLANES = 8
V7X_VMEM_LIMIT_BYTES = 56 * 1024 * 1024

TOKEN_TILE = 512
ATTN_K_TILE = 512
ATTN_Q_TILE = 1024
ATTN_KEY_CHUNK = 16
SUM_ROWS = 16
FF_CHUNK = 1024
LOG2E = 1.4426950408889634

_BF16 = jnp.bfloat16
_F32 = jnp.float32


def _lambda_init(layer_idx):
    return 0.8 - 0.6 * math.exp(-0.3 * layer_idx)


def _rms(x, g):
    return x * lax.rsqrt(jnp.mean(x * x, axis=-1, keepdims=True) + EPS) * g


def _dot(a, b):
    return jnp.dot(a, b, preferred_element_type=_F32)


def _resident(shape, layer=None):
    if layer is None:
        return pl.BlockSpec(shape, lambda *_: (0,) * len(shape),
                            pipeline_mode=pl.Buffered(1))
    return pl.BlockSpec((None,) + shape, lambda *_: (layer,) + (0,) * len(shape),
                        pipeline_mode=pl.Buffered(1))


def _qkv_rope_kernel(x_ref, g_ref, w_ref, ca_ref, cm_ref, cp_ref,
                     qt_ref, k_ref, vt_ref):
    qkv = _dot(_rms(x_ref[...], g_ref[...]), w_ref[...])
    ca, cm, cp = ca_ref[...], cm_ref[...], cp_ref[...]

    def rope(t):
        return (t * ca + pltpu.roll(t, V7X_LANES - ROT_DIM // 2, 1) * cm
                + pltpu.roll(t, ROT_DIM // 2, 1) * cp)

    q_scale = (HEAD_DIM ** -0.5) * LOG2E
    for hd in range(N_HEADS):
        lo, hi = hd * V_DIM, (hd + 1) * V_DIM
        qt_ref[lo:hi, :] = (rope(qkv[:, lo:hi]) * q_scale).T.astype(_BF16)
        k_ref[:, lo:hi] = rope(qkv[:, QK_WIDTH + lo:QK_WIDTH + hi]).astype(_BF16)
        vt_ref[lo:hi, :] = qkv[:, 2 * QK_WIDTH + lo:2 * QK_WIDTH + hi].T.astype(_BF16)


def _qkv_rope_call(x, g, w_qkv, ca, cm, cp):
    B, S, D = x.shape
    tm = TOKEN_TILE
    n_w = w_qkv.shape[-1]
    return pl.pallas_call(
        _qkv_rope_kernel,
        grid=(B, S // tm),
        in_specs=[
            pl.BlockSpec((None, tm, D), lambda b, i: (b, i, 0)),
            _resident((1, D)),
            _resident((D, n_w), layer=0),
            pl.BlockSpec((tm, V_DIM), lambda b, i: (i, 0)),
            pl.BlockSpec((tm, V_DIM), lambda b, i: (i, 0)),
            pl.BlockSpec((tm, V_DIM), lambda b, i: (i, 0)),
        ],
        out_specs=[
            pl.BlockSpec((None, QK_WIDTH, tm), lambda b, i: (b, 0, i)),
            pl.BlockSpec((None, tm, QK_WIDTH), lambda b, i: (b, i, 0)),
            pl.BlockSpec((None, D, tm), lambda b, i: (b, 0, i)),
        ],
        out_shape=[
            jax.ShapeDtypeStruct((B, QK_WIDTH, S), _BF16),
            jax.ShapeDtypeStruct((B, S, QK_WIDTH), _BF16),
            jax.ShapeDtypeStruct((B, D, S), _BF16),
        ],
        compiler_params=pltpu.CompilerParams(
            dimension_semantics=("arbitrary", "arbitrary"),
            vmem_limit_bytes=V7X_VMEM_LIMIT_BYTES),
        name="qkv_rope",
    )(x, g, w_qkv, ca, cm, cp)


def _diff_attn_kernel(lq1_ref, lk1_ref, lq2_ref, lk2_ref, g_ref,
                      qt_ref, k_ref, vt_ref, o_ref,
                      qz_ref, s_ref, x_ref, p_ref, a_ref, m_ref, acc_ref, *, lambda_init):
    tq, tk = ATTN_Q_TILE, ATTN_K_TILE
    kc = ATTN_KEY_CHUNK
    ns = V7X_SUBLANES
    i = pl.program_id(2)
    all_q = slice(0, tq)
    top_q = slice(tk, tq)

    m_ref[...] = jnp.full(m_ref.shape, -jnp.inf, _F32)
    acc_ref[...] = jnp.zeros(acc_ref.shape, _F32)

    qt = qt_ref[...]
    sub = lax.broadcasted_iota(jnp.int32, qt.shape, 0) // HEAD_DIM
    for c in range(2):
        qz_ref[c] = jnp.where(sub == c, qt, jnp.zeros_like(qt))

    def scores(j, slot, qs=all_q, key_off=None):
        nq = qs.stop - qs.start
        k = k_ref[pl.ds(pl.multiple_of(j * tk, tk), tk), :]
        for c in range(2):
            s = _dot(k, qz_ref[c, :, qs])
            if key_off is not None:
                key = lax.broadcasted_iota(jnp.int32, (tk, nq), 0) + key_off
                qry = lax.broadcasted_iota(jnp.int32, (tk, nq), 1) + qs.start
                s = jnp.where(key <= qry, s, -1e30)
            s_ref[slot, c, :, qs] = s
            x_ref[slot, c, :, qs] = jnp.max(s.reshape(tk // ns, ns, nq), axis=0)

    def softmax(slot, qs=all_q):
        for c in range(2):
            m_old = m_ref[c, :, qs]
            m_new = jnp.maximum(m_old, jnp.max(x_ref[slot, c, :, qs], axis=0, keepdims=True))
            a_ref[slot, c, :, qs] = jnp.exp2(m_old - m_new)
            m_ref[c, :, qs] = m_new
            for r in range(tk // kc):
                rows = slice(r * kc, (r + 1) * kc)
                p = jnp.exp2(s_ref[slot, c, rows, qs] - m_new)
                p_ref[slot, c, rows, qs] = p.astype(_BF16)

    def accumulate(j, slot, qs=all_q):
        vt = vt_ref[:, pl.ds(pl.multiple_of(j * tk, tk), tk)]
        vt1 = jnp.concatenate([vt, jnp.ones((SUM_ROWS, tk), _BF16)], axis=0)
        for c in range(2):
            acc_ref[c, :, qs] = (a_ref[slot, c, :, qs] * acc_ref[c, :, qs]
                                 + _dot(vt1, p_ref[slot, c, :, qs]))

    last = 2 * i + 1

    @pl.when(i == 0)
    def _():
        scores(0, 0, key_off=0)
        scores(1, 1, top_q, key_off=tk)
        softmax(0)
        accumulate(0, 0)
        softmax(1, top_q)
        accumulate(1, 1, top_q)

    @pl.when(i > 0)
    def _():
        scores(0, 0)
        scores(1, 1)
        softmax(0)

        def pair(jj, carry):
            j = 2 * jj + 1
            scores(j + 1, 0)
            accumulate(j - 1, 0)
            softmax(1)
            scores(j + 2, 1)
            accumulate(j, 1)
            softmax(0)
            return carry

        lax.fori_loop(0, i - 1, pair, 0)

        scores(last - 1, 0, key_off=0)
        accumulate(last - 3, 0)
        softmax(1)
        scores(last, 1, top_q, key_off=tk)
        accumulate(last - 2, 1)
        softmax(0)
        accumulate(last - 1, 0)
        softmax(1, top_q)
        accumulate(last, 1, top_q)

    lam = (jnp.exp(jnp.sum(lq1_ref[...] * lk1_ref[...]))
           - jnp.exp(jnp.sum(lq2_ref[...] * lk2_ref[...])) + lambda_init)
    l1 = acc_ref[0, V_DIM:V_DIM + 1, :]
    l2 = acc_ref[1, V_DIM:V_DIM + 1, :]
    ot = acc_ref[0, :V_DIM, :] / l1 - lam * (acc_ref[1, :V_DIM, :] / l2)
    ot = ot * lax.rsqrt(jnp.mean(ot * ot, axis=0, keepdims=True) + EPS)
    o = ot.T * g_ref[...] * (1.0 - lambda_init)
    o_ref[...] = o.astype(o_ref.dtype)


def _diff_attn_call(qt, k, vt, lq1, lk1, lq2, lk2, g, lambda_init):
    B, S, _ = k.shape
    tq, tk = ATTN_Q_TILE, ATTN_K_TILE
    lam_spec = pl.BlockSpec((1, HEAD_DIM), lambda b, h, i: (0, 0))
    return pl.pallas_call(
        functools.partial(_diff_attn_kernel, lambda_init=lambda_init),
        grid=(B, N_HEADS, S // tq),
        in_specs=[
            lam_spec, lam_spec, lam_spec, lam_spec,
            pl.BlockSpec((1, V_DIM), lambda b, h, i: (0, 0)),
            pl.BlockSpec((None, V_DIM, tq), lambda b, h, i: (b, h, i)),
            pl.BlockSpec((None, S, V_DIM), lambda b, h, i: (b, 0, h)),
            pl.BlockSpec((None, V_DIM, S), lambda b, h, i: (b, h, 0)),
        ],
        out_specs=pl.BlockSpec((None, tq, V_DIM), lambda b, h, i: (b, i, h)),
        out_shape=jax.ShapeDtypeStruct((B, S, N_HEADS * V_DIM), _BF16),
        scratch_shapes=[
            pltpu.VMEM((2, V_DIM, tq), _BF16),
            pltpu.VMEM((2, 2, tk, tq), _F32),
            pltpu.VMEM((2, 2, V7X_SUBLANES, tq), _F32),
            pltpu.VMEM((2, 2, tk, tq), _BF16),
            pltpu.VMEM((2, 2, 1, tq), _F32),
            pltpu.VMEM((2, 1, tq), _F32),
            pltpu.VMEM((2, V_DIM + SUM_ROWS, tq), _F32),
        ],
        compiler_params=pltpu.CompilerParams(
            dimension_semantics=("arbitrary", "arbitrary", "arbitrary"),
            vmem_limit_bytes=V7X_VMEM_LIMIT_BYTES),
        name="diff_attn",
    )(lq1, lk1, lq2, lk2, g, qt, k, vt)


def _mlp_block(x1, g_pre_ref, w_up_ref, w_down_ref, g_post_ref, act_ref):
    h = _rms(x1, g_pre_ref[...]).astype(_BF16)
    for c in range(D_FF // FF_CHUNK):
        lo, hi = c * FF_CHUNK, (c + 1) * FF_CHUNK
        u = jnp.maximum(_dot(h, w_up_ref[:, lo:hi]), 0.0)
        act_ref[:, lo:hi] = (u * u).astype(_BF16)
    y = _dot(act_ref[...], w_down_ref[...])
    return x1 + _rms(y, g_post_ref[...])


def _attn_out_mlp_kernel(o_ref, x_ref, w_o_ref, g_post_ref, g_mpre_ref,
                         w_up_ref, w_down_ref, g_mpost_ref, out_ref, act_ref):
    m = _dot(o_ref[...], w_o_ref[...])
    x1 = x_ref[...] + _rms(m, g_post_ref[...])
    out_ref[...] = _mlp_block(x1, g_mpre_ref, w_up_ref, w_down_ref,
                              g_mpost_ref, act_ref)


def _attn_out_mlp_call(o, x, w_o, g_post, g_mpre, w_up, w_down, g_mpost, layer):
    T, D = x.shape
    tm = TOKEN_TILE
    row = lambda i: (i, 0)
    return pl.pallas_call(
        _attn_out_mlp_kernel,
        grid=(T // tm,),
        in_specs=[
            pl.BlockSpec((tm, D), row),
            pl.BlockSpec((tm, D), row),
            _resident((D, D)),
            _resident((1, D)),
            _resident((1, D)),
            _resident((D, D_FF), layer),
            _resident((D_FF, D), layer),
            _resident((1, D)),
        ],
        out_specs=pl.BlockSpec((tm, D), row),
        out_shape=jax.ShapeDtypeStruct((T, D), _F32),
        scratch_shapes=[pltpu.VMEM((tm, D_FF), _BF16)],
        compiler_params=pltpu.CompilerParams(
            dimension_semantics=("arbitrary",),
            vmem_limit_bytes=V7X_VMEM_LIMIT_BYTES),
        name="attn_out_mlp",
    )(o, x, w_o, g_post, g_mpre, w_up, w_down, g_mpost)


def _conv_mlp_kernel(x_ref, g_pre_ref, w_in_ref, cw_ref, w_out_ref, g_post_ref,
                     g_mpre_ref, w_up_ref, w_down_ref, g_mpost_ref,
                     out_ref, act_ref, gated_ref, tail_ref, *, tiles_per_seq):
    D = D_MODEL
    x = x_ref[...]
    h = _rms(x, g_pre_ref[...]).astype(_BF16)
    hh = _dot(h, w_in_ref[...])
    b_gate = hh[:, :D]
    u = hh[:, D:2 * D] * hh[:, 2 * D:]

    @pl.when(pl.program_id(0) % tiles_per_seq == 0)
    def _():
        tail_ref[...] = jnp.zeros_like(tail_ref)

    w0, w1, w2 = cw_ref[0:1, :], cw_ref[1:2, :], cw_ref[2:3, :]
    y = w0 * pltpu.roll(u, 2, 0) + w1 * pltpu.roll(u, 1, 0) + w2 * u
    gated_ref[...] = (b_gate * y).astype(_BF16)

    ns = V7X_SUBLANES
    tail = tail_ref[...]
    u_top = u[:ns, :]
    rid = lax.broadcasted_iota(jnp.int32, (ns, D), 0)
    u1 = jnp.where(rid < 1, pltpu.roll(tail, 1, 0), pltpu.roll(u_top, 1, 0))
    u2 = jnp.where(rid < 2, pltpu.roll(tail, 2, 0), pltpu.roll(u_top, 2, 0))
    y_top = w0 * u2 + w1 * u1 + w2 * u_top
    gated_ref[:ns, :] = (b_gate[:ns, :] * y_top).astype(_BF16)
    tail_ref[...] = u[-ns:, :]

    m = _dot(gated_ref[...], w_out_ref[...])
    x1 = x + _rms(m, g_post_ref[...])
    out_ref[...] = _mlp_block(x1, g_mpre_ref, w_up_ref, w_down_ref,
                              g_mpost_ref, act_ref)


def _conv_mlp_call(x, seq_len, g_pre, w_in, conv_w, w_out, g_post,
                   g_mpre, w_up, w_down, g_mpost, layer):
    T, D = x.shape
    tm = TOKEN_TILE
    row = lambda i: (i, 0)
    return pl.pallas_call(
        functools.partial(_conv_mlp_kernel, tiles_per_seq=seq_len // tm),
        grid=(T // tm,),
        in_specs=[
            pl.BlockSpec((tm, D), row),
            _resident((1, D)),
            _resident((D, 3 * D)),
            _resident((CONV_WIDTH, D)),
            _resident((D, D)),
            _resident((1, D)),
            _resident((1, D)),
            _resident((D, D_FF), layer),
            _resident((D_FF, D), layer),
            _resident((1, D)),
        ],
        out_specs=pl.BlockSpec((tm, D), row),
        out_shape=jax.ShapeDtypeStruct((T, D), _F32),
        scratch_shapes=[
            pltpu.VMEM((tm, D_FF), _BF16),
            pltpu.VMEM((tm, D), _BF16),
            pltpu.VMEM((V7X_SUBLANES, D), _F32),
        ],
        compiler_params=pltpu.CompilerParams(
            dimension_semantics=("arbitrary",),
            vmem_limit_bytes=V7X_VMEM_LIMIT_BYTES),
        name="conv_mlp",
    )(x, g_pre, w_in, conv_w, w_out, g_post, g_mpre, w_up, w_down, g_mpost)


def _rope_tables(seq_len):
    half = ROT_DIM // 2
    pos = np.arange(seq_len, dtype=np.float64)
    inv_freq = ROPE_THETA ** (-np.arange(0, ROT_DIM, 2, dtype=np.float64) / ROT_DIM)
    ang = pos[:, None] * inv_freq[None, :]
    cos = np.cos(ang).astype(np.float32)
    sin = np.sin(ang).astype(np.float32)
    ones = np.ones((seq_len, HEAD_DIM - ROT_DIM), np.float32)
    zeros = np.zeros((seq_len, HEAD_DIM - half), np.float32)
    zeros_h = np.zeros((seq_len, half), np.float32)
    zeros_r = np.zeros((seq_len, HEAD_DIM - ROT_DIM), np.float32)
    ca = np.concatenate([cos, cos, ones], axis=-1)
    cm = np.concatenate([-sin, zeros], axis=-1)
    cp = np.concatenate([zeros_h, sin, zeros_r], axis=-1)
    two = lambda a: jnp.asarray(np.concatenate([a, a], axis=-1))
    return two(ca), two(cm), two(cp)


def kernel(x, attn_w_qkv, attn_w_o, attn_lambda_q1, attn_lambda_k1, attn_lambda_q2,
           attn_lambda_k2, attn_subln_g, conv_w_in, conv_w, conv_w_out, mlp_w_up,
           mlp_w_down, norm_mixer_pre, norm_mixer_post, norm_mlp_pre, norm_mlp_post):
    B, S, D = x.shape
    row = lambda a: a.reshape(1, -1)
    bf = lambda a: a.astype(_BF16)

    w_up, w_down = bf(mlp_w_up), bf(mlp_w_down)

    ca, cm, cp = _rope_tables(S)
    qt, k, vt = _qkv_rope_call(x, row(norm_mixer_pre[0]), attn_w_qkv, ca, cm, cp)
    o = _diff_attn_call(qt, k, vt, row(attn_lambda_q1[0]), row(attn_lambda_k1[0]),
                        row(attn_lambda_q2[0]), row(attn_lambda_k2[0]),
                        row(attn_subln_g[0]), _lambda_init(0))
    x2 = _attn_out_mlp_call(
        o.reshape(B * S, D), x.reshape(B * S, D), bf(attn_w_o[0]),
        row(norm_mixer_post[0]), row(norm_mlp_pre[0]), w_up, w_down,
        row(norm_mlp_post[0]), layer=0)

    x3 = _conv_mlp_call(
        x2, S, row(norm_mixer_pre[1]), bf(conv_w_in[0]), conv_w[0], bf(conv_w_out[0]),
        row(norm_mixer_post[1]), row(norm_mlp_pre[1]), w_up, w_down,
        row(norm_mlp_post[1]), layer=1)
    return x3.reshape(B, S, D)
```

```python
import functools
import math

import jax
import jax.numpy as jnp
import numpy as np
from jax import lax
from jax.experimental import pallas as pl
from jax.experimental.pallas import tpu as pltpu

D_MODEL = 1024
HEAD_DIM = 64
V_DIM = 2 * HEAD_DIM
N_HEADS = D_MODEL // V_DIM
QK_WIDTH = N_HEADS * 2 * HEAD_DIM
ROT_DIM = HEAD_DIM // 4
ROPE_THETA = 500000.0
CONV_WIDTH = 3
D_FF = 4 * D_MODEL
EPS = 1e-6

V7X_LANES = 128
V7X_SUBLANES = 8
V7X_MXU_DIM = 256
V7X_VMEM_LIMIT_BYTES = 56 * 1024 * 1024

TOKEN_TILE = 512
ATTN_K_TILE = 512
ATTN_Q_TILE = 1024
SUM_ROWS = 16
FF_CHUNK = 1024
LOG2E = 1.4426950408889634

_BF16 = jnp.bfloat16
_F32 = jnp.float32


def _lambda_init(layer_idx):
    return 0.8 - 0.6 * math.exp(-0.3 * layer_idx)


def _rms(x, g):
    return x * lax.rsqrt(jnp.mean(x * x, axis=-1, keepdims=True) + EPS) * g


def _dot(a, b):
    return jnp.dot(a, b, preferred_element_type=_F32)


def _resident(shape, layer=None):
    if layer is None:
        return pl.BlockSpec(shape, lambda *_: (0,) * len(shape),
                            pipeline_mode=pl.Buffered(1))
    return pl.BlockSpec((None,) + shape, lambda *_: (layer,) + (0,) * len(shape),
                        pipeline_mode=pl.Buffered(1))


def _qkv_rope_kernel(x_ref, g_ref, w_ref, ca_ref, cm_ref, cp_ref,
                     qt_ref, k_ref, vt_ref):
    qkv = _dot(_rms(x_ref[...], g_ref[...]), w_ref[...])
    ca, cm, cp = ca_ref[...], cm_ref[...], cp_ref[...]

    def rope(t):
        return (t * ca + pltpu.roll(t, V7X_LANES - ROT_DIM // 2, 1) * cm
                + pltpu.roll(t, ROT_DIM // 2, 1) * cp)

    q_scale = (HEAD_DIM ** -0.5) * LOG2E
    for hd in range(N_HEADS):
        lo, hi = hd * V_DIM, (hd + 1) * V_DIM
        qt_ref[lo:hi, :] = (rope(qkv[:, lo:hi]) * q_scale).T.astype(_BF16)
        k_ref[:, lo:hi] = rope(qkv[:, QK_WIDTH + lo:QK_WIDTH + hi]).astype(_BF16)
        vt_ref[lo:hi, :] = qkv[:, 2 * QK_WIDTH + lo:2 * QK_WIDTH + hi].T.astype(_BF16)


def _qkv_rope_call(x, g, w_qkv, ca, cm, cp):
    B, S, D = x.shape
    tm = TOKEN_TILE
    n_w = w_qkv.shape[-1]
    return pl.pallas_call(
        _qkv_rope_kernel,
        grid=(B, S // tm),
        in_specs=[
            pl.BlockSpec((None, tm, D), lambda b, i: (b, i, 0)),
            _resident((1, D)),
            _resident((D, n_w), layer=0),
            pl.BlockSpec((tm, V_DIM), lambda b, i: (i, 0)),
            pl.BlockSpec((tm, V_DIM), lambda b, i: (i, 0)),
            pl.BlockSpec((tm, V_DIM), lambda b, i: (i, 0)),
        ],
        out_specs=[
            pl.BlockSpec((None, QK_WIDTH, tm), lambda b, i: (b, 0, i)),
            pl.BlockSpec((None, tm, QK_WIDTH), lambda b, i: (b, i, 0)),
            pl.BlockSpec((None, D, tm), lambda b, i: (b, 0, i)),
        ],
        out_shape=[
            jax.ShapeDtypeStruct((B, QK_WIDTH, S), _BF16),
            jax.ShapeDtypeStruct((B, S, QK_WIDTH), _BF16),
            jax.ShapeDtypeStruct((B, D, S), _BF16),
        ],
        compiler_params=pltpu.CompilerParams(
            dimension_semantics=("arbitrary", "arbitrary"),
            vmem_limit_bytes=V7X_VMEM_LIMIT_BYTES),
        name="qkv_rope",
    )(x, g, w_qkv, ca, cm, cp)


def _diff_attn_kernel(par_ref, qt_ref, k_ref, vt_ref, o_ref,
                      qz_ref, s_ref, x_ref, m_ref, acc_ref, *, lambda_init):
    tq, tk = ATTN_Q_TILE, ATTN_K_TILE
    ns = V7X_SUBLANES
    i = pl.program_id(2)
    blk = V7X_MXU_DIM

    m_ref[...] = jnp.full(m_ref.shape, -jnp.inf, _F32)
    acc_ref[...] = jnp.zeros(acc_ref.shape, _F32)

    qt = qt_ref[...]
    sub = lax.broadcasted_iota(jnp.int32, qt.shape, 0) // HEAD_DIM
    for c in range(2):
        qz_ref[c] = jnp.where(sub == c, qt, jnp.zeros_like(qt))

    def score_block(j, slot, c, n, key_off):
        cols = slice(n, n + blk)
        k = k_ref[pl.ds(pl.multiple_of(j * tk, tk), tk), :]
        s = _dot(k, qz_ref[c, :, cols])
        if key_off is not None:
            key = lax.broadcasted_iota(jnp.int32, (tk, blk), 0) + key_off
            qry = lax.broadcasted_iota(jnp.int32, (tk, blk), 1) + n
            s = jnp.where(key <= qry, s, -1e30)
        s_ref[slot, c, :, cols] = s
        x_ref[slot, c, :, cols] = jnp.max(s.reshape(tk // ns, ns, blk), axis=0)

    def consume_block(j, slot, c, n):
        cols = slice(n, n + blk)
        m_old = m_ref[c, :, cols]
        m_new = jnp.maximum(m_old, jnp.max(x_ref[slot, c, :, cols], axis=0, keepdims=True))
        m_ref[c, :, cols] = m_new
        pv = None
        for kh in range(tk // blk):
            keys = slice(kh * blk, (kh + 1) * blk)
            vt = vt_ref[:, pl.ds(pl.multiple_of(j * tk + kh * blk, blk), blk)]
            vt1 = jnp.concatenate([vt, jnp.ones((SUM_ROWS, blk), _BF16)], axis=0)
            p = jnp.exp2(s_ref[slot, c, keys, cols] - m_new).astype(_BF16)
            part = _dot(vt1, p)
            pv = part if pv is None else pv + part
        acc_ref[c, :, cols] = jnp.exp2(m_old - m_new) * acc_ref[c, :, cols] + pv

    def stage(score=None, consume=None):
        for c in range(2):
            for n in range(0, tq, blk):
                if score is not None and n >= score[2]:
                    score_block(score[0], score[1], c, n, score[3])
                if consume is not None and n >= consume[2]:
                    consume_block(consume[0], consume[1], c, n)

    last = 2 * i + 1

    @pl.when(i == 0)
    def _():
        stage(score=(0, 0, 0, 0))
        stage(score=(1, 1, tk, tk), consume=(0, 0, 0))
        stage(consume=(1, 1, tk))

    @pl.when(i > 0)
    def _():
        stage(score=(0, 0, 0, None))
        stage(score=(1, 1, 0, None), consume=(0, 0, 0))

        def pair(jj, carry):
            j = 2 * jj + 1
            stage(score=(j + 1, 0, 0, None), consume=(j, 1, 0))
            stage(score=(j + 2, 1, 0, None), consume=(j + 1, 0, 0))
            return carry

        lax.fori_loop(0, i - 1, pair, 0)

        stage(score=(last - 1, 0, 0, 0), consume=(last - 2, 1, 0))
        stage(score=(last, 1, tk, tk), consume=(last - 1, 0, 0))
        stage(consume=(last, 1, tk))

    lam = (jnp.exp(jnp.sum(par_ref[0:1, :] * par_ref[1:2, :]))
           - jnp.exp(jnp.sum(par_ref[2:3, :] * par_ref[3:4, :])) + lambda_init)
    l1 = acc_ref[0, V_DIM:V_DIM + 1, :]
    l2 = acc_ref[1, V_DIM:V_DIM + 1, :]
    ot = acc_ref[0, :V_DIM, :] / l1 - lam * (acc_ref[1, :V_DIM, :] / l2)
    ot = ot * lax.rsqrt(jnp.mean(ot * ot, axis=0, keepdims=True) + EPS)
    o = ot.T * par_ref[4:5, :] * (1.0 - lambda_init)
    o_ref[...] = o.astype(o_ref.dtype)


def _diff_attn_call(qt, k, vt, lq1, lk1, lq2, lk2, g, lambda_init):
    B, S, _ = k.shape
    tq, tk = ATTN_Q_TILE, ATTN_K_TILE
    pad = lambda a: jnp.pad(a.reshape(1, -1), ((0, 0), (0, V7X_LANES - a.size)))
    par = jnp.concatenate([pad(lq1), pad(lk1), pad(lq2), pad(lk2), pad(g),
                           jnp.zeros((V7X_SUBLANES - 5, V7X_LANES), _F32)], axis=0)
    return pl.pallas_call(
        functools.partial(_diff_attn_kernel, lambda_init=lambda_init),
        grid=(B, N_HEADS, S // tq),
        in_specs=[
            _resident((V7X_SUBLANES, V7X_LANES)),
            pl.BlockSpec((None, V_DIM, tq), lambda b, h, i: (b, h, i)),
            pl.BlockSpec((None, S, V_DIM), lambda b, h, i: (b, 0, h)),
            pl.BlockSpec((None, V_DIM, S), lambda b, h, i: (b, h, 0)),
        ],
        out_specs=pl.BlockSpec((None, tq, V_DIM), lambda b, h, i: (b, i, h)),
        out_shape=jax.ShapeDtypeStruct((B, S, N_HEADS * V_DIM), _BF16),
        scratch_shapes=[
            pltpu.VMEM((2, V_DIM, tq), _BF16),
            pltpu.VMEM((2, 2, tk, tq), _F32),
            pltpu.VMEM((2, 2, V7X_SUBLANES, tq), _F32),
            pltpu.VMEM((2, 1, tq), _F32),
            pltpu.VMEM((2, V_DIM + SUM_ROWS, tq), _F32),
        ],
        compiler_params=pltpu.CompilerParams(
            dimension_semantics=("arbitrary", "arbitrary", "arbitrary"),
            vmem_limit_bytes=V7X_VMEM_LIMIT_BYTES),
        name="diff_attn",
    )(par, qt, k, vt)


def _mlp_block(x1, g_pre_ref, w_up_ref, w_down_ref, g_post_ref, act_ref):
    h = _rms(x1, g_pre_ref[...]).astype(_BF16)
    for c in range(D_FF // FF_CHUNK):
        lo, hi = c * FF_CHUNK, (c + 1) * FF_CHUNK
        u = jnp.maximum(_dot(h, w_up_ref[:, lo:hi]), 0.0)
        act_ref[:, lo:hi] = (u * u).astype(_BF16)
    y = _dot(act_ref[...], w_down_ref[...])
    return x1 + _rms(y, g_post_ref[...])


def _attn_out_mlp_kernel(o_ref, x_ref, w_o_ref, g_post_ref, g_mpre_ref,
                         w_up_ref, w_down_ref, g_mpost_ref, out_ref, act_ref):
    m = _dot(o_ref[...], w_o_ref[...])
    x1 = x_ref[...] + _rms(m, g_post_ref[...])
    out_ref[...] = _mlp_block(x1, g_mpre_ref, w_up_ref, w_down_ref,
                              g_mpost_ref, act_ref)


def _attn_out_mlp_call(o, x, w_o, g_post, g_mpre, w_up, w_down, g_mpost, layer):
    T, D = x.shape
    tm = TOKEN_TILE
    row = lambda i: (i, 0)
    return pl.pallas_call(
        _attn_out_mlp_kernel,
        grid=(T // tm,),
        in_specs=[
            pl.BlockSpec((tm, D), row),
            pl.BlockSpec((tm, D), row),
            _resident((D, D)),
            _resident((1, D)),
            _resident((1, D)),
            _resident((D, D_FF), layer),
            _resident((D_FF, D), layer),
            _resident((1, D)),
        ],
        out_specs=pl.BlockSpec((tm, D), row),
        out_shape=jax.ShapeDtypeStruct((T, D), _F32),
        scratch_shapes=[pltpu.VMEM((tm, D_FF), _BF16)],
        compiler_params=pltpu.CompilerParams(
            dimension_semantics=("arbitrary",),
            vmem_limit_bytes=V7X_VMEM_LIMIT_BYTES),
        name="attn_out_mlp",
    )(o, x, w_o, g_post, g_mpre, w_up, w_down, g_mpost)


def _conv_mlp_kernel(x_ref, g_pre_ref, w_in_ref, cw_ref, w_out_ref, g_post_ref,
                     g_mpre_ref, w_up_ref, w_down_ref, g_mpost_ref,
                     out_ref, act_ref, gated_ref, tail_ref, *, tiles_per_seq):
    D = D_MODEL
    x = x_ref[...]
    h = _rms(x, g_pre_ref[...]).astype(_BF16)
    hh = _dot(h, w_in_ref[...])
    b_gate = hh[:, :D]
    u = hh[:, D:2 * D] * hh[:, 2 * D:]

    @pl.when(pl.program_id(0) % tiles_per_seq == 0)
    def _():
        tail_ref[...] = jnp.zeros_like(tail_ref)

    w0, w1, w2 = cw_ref[0:1, :], cw_ref[1:2, :], cw_ref[2:3, :]
    y = w0 * pltpu.roll(u, 2, 0) + w1 * pltpu.roll(u, 1, 0) + w2 * u
    gated_ref[...] = (b_gate * y).astype(_BF16)

    ns = V7X_SUBLANES
    tail = tail_ref[...]
    u_top = u[:ns, :]
    rid = lax.broadcasted_iota(jnp.int32, (ns, D), 0)
    u1 = jnp.where(rid < 1, pltpu.roll(tail, 1, 0), pltpu.roll(u_top, 1, 0))
    u2 = jnp.where(rid < 2, pltpu.roll(tail, 2, 0), pltpu.roll(u_top, 2, 0))
    y_top = w0 * u2 + w1 * u1 + w2 * u_top
    gated_ref[:ns, :] = (b_gate[:ns, :] * y_top).astype(_BF16)
    tail_ref[...] = u[-ns:, :]

    m = _dot(gated_ref[...], w_out_ref[...])
    x1 = x + _rms(m, g_post_ref[...])
    out_ref[...] = _mlp_block(x1, g_mpre_ref, w_up_ref, w_down_ref,
                              g_mpost_ref, act_ref)


def _conv_mlp_call(x, seq_len, g_pre, w_in, conv_w, w_out, g_post,
                   g_mpre, w_up, w_down, g_mpost, layer):
    T, D = x.shape
    tm = TOKEN_TILE
    row = lambda i: (i, 0)
    return pl.pallas_call(
        functools.partial(_conv_mlp_kernel, tiles_per_seq=seq_len // tm),
        grid=(T // tm,),
        in_specs=[
            pl.BlockSpec((tm, D), row),
            _resident((1, D)),
            _resident((D, 3 * D)),
            _resident((CONV_WIDTH, D)),
            _resident((D, D)),
            _resident((1, D)),
            _resident((1, D)),
            _resident((D, D_FF), layer),
            _resident((D_FF, D), layer),
            _resident((1, D)),
        ],
        out_specs=pl.BlockSpec((tm, D), row),
        out_shape=jax.ShapeDtypeStruct((T, D), _F32),
        scratch_shapes=[
            pltpu.VMEM((tm, D_FF), _BF16),
            pltpu.VMEM((tm, D), _BF16),
            pltpu.VMEM((V7X_SUBLANES, D), _F32),
        ],
        compiler_params=pltpu.CompilerParams(
            dimension_semantics=("arbitrary",),
            vmem_limit_bytes=V7X_VMEM_LIMIT_BYTES),
        name="conv_mlp",
    )(x, g_pre, w_in, conv_w, w_out, g_post, g_mpre, w_up, w_down, g_mpost)


def _rope_tables(seq_len):
    half = ROT_DIM // 2
    pos = np.arange(seq_len, dtype=np.float64)
    inv_freq = ROPE_THETA ** (-np.arange(0, ROT_DIM, 2, dtype=np.float64) / ROT_DIM)
    ang = pos[:, None] * inv_freq[None, :]
    cos = np.cos(ang).astype(np.float32)
    sin = np.sin(ang).astype(np.float32)
    ones = np.ones((seq_len, HEAD_DIM - ROT_DIM), np.float32)
    zeros = np.zeros((seq_len, HEAD_DIM - half), np.float32)
    zeros_h = np.zeros((seq_len, half), np.float32)
    zeros_r = np.zeros((seq_len, HEAD_DIM - ROT_DIM), np.float32)
    ca = np.concatenate([cos, cos, ones], axis=-1)
    cm = np.concatenate([-sin, zeros], axis=-1)
    cp = np.concatenate([zeros_h, sin, zeros_r], axis=-1)
    two = lambda a: jnp.asarray(np.concatenate([a, a], axis=-1))
    return two(ca), two(cm), two(cp)


def kernel(x, attn_w_qkv, attn_w_o, attn_lambda_q1, attn_lambda_k1, attn_lambda_q2,
           attn_lambda_k2, attn_subln_g, conv_w_in, conv_w, conv_w_out, mlp_w_up,
           mlp_w_down, norm_mixer_pre, norm_mixer_post, norm_mlp_pre, norm_mlp_post):
    B, S, D = x.shape
    row = lambda a: a.reshape(1, -1)
    bf = lambda a: a.astype(_BF16)

    w_up, w_down = bf(mlp_w_up), bf(mlp_w_down)

    ca, cm, cp = _rope_tables(S)
    qt, k, vt = _qkv_rope_call(x, row(norm_mixer_pre[0]), attn_w_qkv, ca, cm, cp)
    o = _diff_attn_call(qt, k, vt, row(attn_lambda_q1[0]), row(attn_lambda_k1[0]),
                        row(attn_lambda_q2[0]), row(attn_lambda_k2[0]),
                        row(attn_subln_g[0]), _lambda_init(0))
    x2 = _attn_out_mlp_call(
        o.reshape(B * S, D), x.reshape(B * S, D), bf(attn_w_o[0]),
        row(norm_mixer_post[0]), row(norm_mlp_pre[0]), w_up, w_down,
        row(norm_mlp_post[0]), layer=0)

    x3 = _conv_mlp_call(
        x2, S, row(norm_mixer_pre[1]), bf(conv_w_in[0]), conv_w[0], bf(conv_w_out[0]),
        row(norm_mixer_post[1]), row(norm_mlp_pre[1]), w_up, w_down,
        row(norm_mlp_post[1]), layer=1)
    return x3.reshape(B, S, D)
```

```python
import functools
import math

import jax
import jax.numpy as jnp
import numpy as np
from jax import lax
from jax.experimental import pallas as pl
from jax.experimental.pallas import tpu as pltpu

D_MODEL = 1024
HEAD_DIM = 64
V_DIM = 2 * HEAD_DIM
N_HEADS = D_MODEL // V_DIM
QK_WIDTH = N_HEADS * 2 * HEAD_DIM
ROT_DIM = HEAD_DIM // 4
ROPE_THETA = 500000.0
CONV_WIDTH = 3
D_FF = 4 * D_MODEL
EPS = 1e-6

V7X_LANES = 128
V7X_SUBLANES = 8
V7X_MXU_DIM = 256
V7X_VMEM_LIMIT_BYTES = 56 * 1024 * 1024

TOKEN_TILE = 512
ATTN_K_TILE = 512
ATTN_Q_TILE = 1024
SUM_ROWS = 16
FF_CHUNK = 1024
LOG2E = 1.4426950408889634

_BF16 = jnp.bfloat16
_F32 = jnp.float32


def _lambda_init(layer_idx):
    return 0.8 - 0.6 * math.exp(-0.3 * layer_idx)


def _rms(x, g):
    return x * lax.rsqrt(jnp.mean(x * x, axis=-1, keepdims=True) + EPS) * g


def _dot(a, b):
    return jnp.dot(a, b, preferred_element_type=_F32)


def _resident(shape, layer=None):
    if layer is None:
        return pl.BlockSpec(shape, lambda *_: (0,) * len(shape),
                            pipeline_mode=pl.Buffered(1))
    return pl.BlockSpec((None,) + shape, lambda *_: (layer,) + (0,) * len(shape),
                        pipeline_mode=pl.Buffered(1))


def _qkv_rope_kernel(x_ref, g_ref, w_ref, ca_ref, cm_ref, cp_ref,
                     qt_ref, k_ref, vt_ref):
    qkv = _dot(_rms(x_ref[...], g_ref[...]), w_ref[...])
    ca, cm, cp = ca_ref[...], cm_ref[...], cp_ref[...]

    def rope(t):
        return (t * ca + pltpu.roll(t, V7X_LANES - ROT_DIM // 2, 1) * cm
                + pltpu.roll(t, ROT_DIM // 2, 1) * cp)

    q_scale = (HEAD_DIM ** -0.5) * LOG2E
    for hd in range(N_HEADS):
        lo, hi = hd * V_DIM, (hd + 1) * V_DIM
        qt_ref[lo:hi, :] = (rope(qkv[:, lo:hi]) * q_scale).T.astype(_BF16)
        k_ref[:, lo:hi] = rope(qkv[:, QK_WIDTH + lo:QK_WIDTH + hi]).astype(_BF16)
        vt_ref[lo:hi, :] = qkv[:, 2 * QK_WIDTH + lo:2 * QK_WIDTH + hi].T.astype(_BF16)


def _qkv_rope_call(x, g, w_qkv, ca, cm, cp):
    B, S, D = x.shape
    tm = TOKEN_TILE
    n_w = w_qkv.shape[-1]
    return pl.pallas_call(
        _qkv_rope_kernel,
        grid=(B, S // tm),
        in_specs=[
            pl.BlockSpec((None, tm, D), lambda b, i: (b, i, 0)),
            _resident((1, D)),
            _resident((D, n_w), layer=0),
            pl.BlockSpec((tm, V_DIM), lambda b, i: (i, 0)),
            pl.BlockSpec((tm, V_DIM), lambda b, i: (i, 0)),
            pl.BlockSpec((tm, V_DIM), lambda b, i: (i, 0)),
        ],
        out_specs=[
            pl.BlockSpec((None, QK_WIDTH, tm), lambda b, i: (b, 0, i)),
            pl.BlockSpec((None, tm, QK_WIDTH), lambda b, i: (b, i, 0)),
            pl.BlockSpec((None, D, tm), lambda b, i: (b, 0, i)),
        ],
        out_shape=[
            jax.ShapeDtypeStruct((B, QK_WIDTH, S), _BF16),
            jax.ShapeDtypeStruct((B, S, QK_WIDTH), _BF16),
            jax.ShapeDtypeStruct((B, D, S), _BF16),
        ],
        compiler_params=pltpu.CompilerParams(
            dimension_semantics=("arbitrary", "arbitrary"),
            vmem_limit_bytes=V7X_VMEM_LIMIT_BYTES),
        name="qkv_rope",
    )(x, g, w_qkv, ca, cm, cp)


def _diff_attn_kernel(par_ref, qt_ref, k_ref, vt_ref, o_ref,
                      qz_ref, s_ref, x_ref, m_ref, acc_ref, *, lambda_init):
    tq, tk = ATTN_Q_TILE, ATTN_K_TILE
    ns = V7X_SUBLANES
    i = pl.program_id(2)
    blk = V7X_MXU_DIM

    m_ref[...] = jnp.full(m_ref.shape, -jnp.inf, _F32)
    acc_ref[...] = jnp.zeros(acc_ref.shape, _F32)

    qt = qt_ref[...]
    sub = lax.broadcasted_iota(jnp.int32, qt.shape, 0) // HEAD_DIM
    for c in range(2):
        qz_ref[c] = jnp.where(sub == c, qt, jnp.zeros_like(qt))

    def live_keys(n, key_off):
        return tk if key_off is None else min(tk, n + blk - key_off)

    def score_block(j, slot, c, n, key_off):
        cols = slice(n, n + blk)
        nk = live_keys(n, key_off)
        k = k_ref[pl.ds(pl.multiple_of(j * tk, tk), nk), :]
        s = _dot(k, qz_ref[c, :, cols])
        if key_off is not None:
            key = lax.broadcasted_iota(jnp.int32, (nk, blk), 0) + key_off
            qry = lax.broadcasted_iota(jnp.int32, (nk, blk), 1) + n
            s = jnp.where(key <= qry, s, -1e30)
        s_ref[slot, c, 0:nk, cols] = s
        x_ref[slot, c, :, cols] = jnp.max(s.reshape(nk // ns, ns, blk), axis=0)

    def consume_block(j, slot, c, n, key_off):
        cols = slice(n, n + blk)
        m_old = m_ref[c, :, cols]
        m_new = jnp.maximum(m_old, jnp.max(x_ref[slot, c, :, cols], axis=0, keepdims=True))
        m_ref[c, :, cols] = m_new
        pv = None
        for kh in range(live_keys(n, key_off) // blk):
            keys = slice(kh * blk, (kh + 1) * blk)
            vt = vt_ref[:, pl.ds(pl.multiple_of(j * tk + kh * blk, blk), blk)]
            vt1 = jnp.concatenate([vt, jnp.ones((SUM_ROWS, blk), _BF16)], axis=0)
            p = jnp.exp2(s_ref[slot, c, keys, cols] - m_new).astype(_BF16)
            part = _dot(vt1, p)
            pv = part if pv is None else pv + part
        acc_ref[c, :, cols] = jnp.exp2(m_old - m_new) * acc_ref[c, :, cols] + pv

    def stage(score=None, consume=None):
        for c in range(2):
            for n in range(0, tq, blk):
                if consume is not None and n >= consume[2]:
                    consume_block(consume[0], consume[1], c, n, consume[3])
                if score is not None and n >= score[2]:
                    score_block(score[0], score[1], c, n, score[3])

    last = 2 * i + 1

    @pl.when(i == 0)
    def _():
        stage(score=(0, 0, 0, 0))
        stage(score=(1, 1, tk, tk), consume=(0, 0, 0, 0))
        stage(consume=(1, 1, tk, tk))

    @pl.when(i > 0)
    def _():
        stage(score=(0, 0, 0, None))
        stage(score=(1, 1, 0, None), consume=(0, 0, 0, None))

        def pair(jj, carry):
            j = 2 * jj + 1
            stage(score=(j + 1, 0, 0, None), consume=(j, 1, 0, None))
            stage(score=(j + 2, 1, 0, None), consume=(j + 1, 0, 0, None))
            return carry

        lax.fori_loop(0, i - 1, pair, 0)

        stage(score=(last - 1, 0, 0, 0), consume=(last - 2, 1, 0, None))
        stage(score=(last, 1, tk, tk), consume=(last - 1, 0, 0, 0))
        stage(consume=(last, 1, tk, tk))

    lam = (jnp.exp(jnp.sum(par_ref[0:1, :] * par_ref[1:2, :]))
           - jnp.exp(jnp.sum(par_ref[2:3, :] * par_ref[3:4, :])) + lambda_init)
    l1 = acc_ref[0, V_DIM:V_DIM + 1, :]
    l2 = acc_ref[1, V_DIM:V_DIM + 1, :]
    ot = acc_ref[0, :V_DIM, :] / l1 - lam * (acc_ref[1, :V_DIM, :] / l2)
    ot = ot * lax.rsqrt(jnp.mean(ot * ot, axis=0, keepdims=True) + EPS)
    o = ot.T * par_ref[4:5, :] * (1.0 - lambda_init)
    o_ref[...] = o.astype(o_ref.dtype)


def _diff_attn_call(qt, k, vt, lq1, lk1, lq2, lk2, g, lambda_init):
    B, S, _ = k.shape
    tq, tk = ATTN_Q_TILE, ATTN_K_TILE
    pad = lambda a: jnp.pad(a.reshape(1, -1), ((0, 0), (0, V7X_LANES - a.size)))
    par = jnp.concatenate([pad(lq1), pad(lk1), pad(lq2), pad(lk2), pad(g),
                           jnp.zeros((V7X_SUBLANES - 5, V7X_LANES), _F32)], axis=0)
    return pl.pallas_call(
        functools.partial(_diff_attn_kernel, lambda_init=lambda_init),
        grid=(B, N_HEADS, S // tq),
        in_specs=[
            _resident((V7X_SUBLANES, V7X_LANES)),
            pl.BlockSpec((None, V_DIM, tq), lambda b, h, i: (b, h, i)),
            pl.BlockSpec((None, S, V_DIM), lambda b, h, i: (b, 0, h)),
            pl.BlockSpec((None, V_DIM, S), lambda b, h, i: (b, h, 0)),
        ],
        out_specs=pl.BlockSpec((None, tq, V_DIM), lambda b, h, i: (b, i, h)),
        out_shape=jax.ShapeDtypeStruct((B, S, N_HEADS * V_DIM), _BF16),
        scratch_shapes=[
            pltpu.VMEM((2, V_DIM, tq), _BF16),
            pltpu.VMEM((2, 2, tk, tq), _F32),
            pltpu.VMEM((2, 2, V7X_SUBLANES, tq), _F32),
            pltpu.VMEM((2, 1, tq), _F32),
            pltpu.VMEM((2, V_DIM + SUM_ROWS, tq), _F32),
        ],
        compiler_params=pltpu.CompilerParams(
            dimension_semantics=("arbitrary", "arbitrary", "arbitrary"),
            vmem_limit_bytes=V7X_VMEM_LIMIT_BYTES),
        name="diff_attn",
    )(par, qt, k, vt)


def _mlp_block(x1, g_pre_ref, w_up_ref, w_down_ref, g_post_ref, act_ref):
    h = _rms(x1, g_pre_ref[...]).astype(_BF16)
    for c in range(D_FF // FF_CHUNK):
        lo, hi = c * FF_CHUNK, (c + 1) * FF_CHUNK
        u = jnp.maximum(_dot(h, w_up_ref[:, lo:hi]), 0.0)
        act_ref[:, lo:hi] = (u * u).astype(_BF16)
    y = _dot(act_ref[...], w_down_ref[...])
    return x1 + _rms(y, g_post_ref[...])


def _attn_out_mlp_kernel(o_ref, x_ref, w_o_ref, g_post_ref, g_mpre_ref,
                         w_up_ref, w_down_ref, g_mpost_ref, out_ref, act_ref):
    m = _dot(o_ref[...], w_o_ref[...])
    x1 = x_ref[...] + _rms(m, g_post_ref[...])
    out_ref[...] = _mlp_block(x1, g_mpre_ref, w_up_ref, w_down_ref,
                              g_mpost_ref, act_ref)


def _attn_out_mlp_call(o, x, w_o, g_post, g_mpre, w_up, w_down, g_mpost, layer):
    T, D = x.shape
    tm = TOKEN_TILE
    row = lambda i: (i, 0)
    return pl.pallas_call(
        _attn_out_mlp_kernel,
        grid=(T // tm,),
        in_specs=[
            pl.BlockSpec((tm, D), row),
            pl.BlockSpec((tm, D), row),
            _resident((D, D)),
            _resident((1, D)),
            _resident((1, D)),
            _resident((D, D_FF), layer),
            _resident((D_FF, D), layer),
            _resident((1, D)),
        ],
        out_specs=pl.BlockSpec((tm, D), row),
        out_shape=jax.ShapeDtypeStruct((T, D), _F32),
        scratch_shapes=[pltpu.VMEM((tm, D_FF), _BF16)],
        compiler_params=pltpu.CompilerParams(
            dimension_semantics=("arbitrary",),
            vmem_limit_bytes=V7X_VMEM_LIMIT_BYTES),
        name="attn_out_mlp",
    )(o, x, w_o, g_post, g_mpre, w_up, w_down, g_mpost)


def _conv_mlp_kernel(x_ref, g_pre_ref, w_in_ref, cw_ref, w_out_ref, g_post_ref,
                     g_mpre_ref, w_up_ref, w_down_ref, g_mpost_ref,
                     out_ref, act_ref, gated_ref, tail_ref, *, tiles_per_seq):
    D = D_MODEL
    x = x_ref[...]
    h = _rms(x, g_pre_ref[...]).astype(_BF16)
    hh = _dot(h, w_in_ref[...])
    b_gate = hh[:, :D]
    u = hh[:, D:2 * D] * hh[:, 2 * D:]

    @pl.when(pl.program_id(0) % tiles_per_seq == 0)
    def _():
        tail_ref[...] = jnp.zeros_like(tail_ref)

    w0, w1, w2 = cw_ref[0:1, :], cw_ref[1:2, :], cw_ref[2:3, :]
    y = w0 * pltpu.roll(u, 2, 0) + w1 * pltpu.roll(u, 1, 0) + w2 * u
    gated_ref[...] = (b_gate * y).astype(_BF16)

    ns = V7X_SUBLANES
    tail = tail_ref[...]
    u_top = u[:ns, :]
    rid = lax.broadcasted_iota(jnp.int32, (ns, D), 0)
    u1 = jnp.where(rid < 1, pltpu.roll(tail, 1, 0), pltpu.roll(u_top, 1, 0))
    u2 = jnp.where(rid < 2, pltpu.roll(tail, 2, 0), pltpu.roll(u_top, 2, 0))
    y_top = w0 * u2 + w1 * u1 + w2 * u_top
    gated_ref[:ns, :] = (b_gate[:ns, :] * y_top).astype(_BF16)
    tail_ref[...] = u[-ns:, :]

    m = _dot(gated_ref[...], w_out_ref[...])
    x1 = x + _rms(m, g_post_ref[...])
    out_ref[...] = _mlp_block(x1, g_mpre_ref, w_up_ref, w_down_ref,
                              g_mpost_ref, act_ref)


def _conv_mlp_call(x, seq_len, g_pre, w_in, conv_w, w_out, g_post,
                   g_mpre, w_up, w_down, g_mpost, layer):
    T, D = x.shape
    tm = TOKEN_TILE
    row = lambda i: (i, 0)
    return pl.pallas_call(
        functools.partial(_conv_mlp_kernel, tiles_per_seq=seq_len // tm),
        grid=(T // tm,),
        in_specs=[
            pl.BlockSpec((tm, D), row),
            _resident((1, D)),
            _resident((D, 3 * D)),
            _resident((CONV_WIDTH, D)),
            _resident((D, D)),
            _resident((1, D)),
            _resident((1, D)),
            _resident((D, D_FF), layer),
            _resident((D_FF, D), layer),
            _resident((1, D)),
        ],
        out_specs=pl.BlockSpec((tm, D), row),
        out_shape=jax.ShapeDtypeStruct((T, D), _F32),
        scratch_shapes=[
            pltpu.VMEM((tm, D_FF), _BF16),
            pltpu.VMEM((tm, D), _BF16),
            pltpu.VMEM((V7X_SUBLANES, D), _F32),
        ],
        compiler_params=pltpu.CompilerParams(
            dimension_semantics=("arbitrary",),
            vmem_limit_bytes=V7X_VMEM_LIMIT_BYTES),
        name="conv_mlp",
    )(x, g_pre, w_in, conv_w, w_out, g_post, g_mpre, w_up, w_down, g_mpost)


def _rope_tables(seq_len):
    half = ROT_DIM // 2
    pos = np.arange(seq_len, dtype=np.float64)
    inv_freq = ROPE_THETA ** (-np.arange(0, ROT_DIM, 2, dtype=np.float64) / ROT_DIM)
    ang = pos[:, None] * inv_freq[None, :]
    cos = np.cos(ang).astype(np.float32)
    sin = np.sin(ang).astype(np.float32)
    ones = np.ones((seq_len, HEAD_DIM - ROT_DIM), np.float32)
    zeros = np.zeros((seq_len, HEAD_DIM - half), np.float32)
    zeros_h = np.zeros((seq_len, half), np.float32)
    zeros_r = np.zeros((seq_len, HEAD_DIM - ROT_DIM), np.float32)
    ca = np.concatenate([cos, cos, ones], axis=-1)
    cm = np.concatenate([-sin, zeros], axis=-1)
    cp = np.concatenate([zeros_h, sin, zeros_r], axis=-1)
    two = lambda a: jnp.asarray(np.concatenate([a, a], axis=-1))
    return two(ca), two(cm), two(cp)


def kernel(x, attn_w_qkv, attn_w_o, attn_lambda_q1, attn_lambda_k1, attn_lambda_q2,
           attn_lambda_k2, attn_subln_g, conv_w_in, conv_w, conv_w_out, mlp_w_up,
           mlp_w_down, norm_mixer_pre, norm_mixer_post, norm_mlp_pre, norm_mlp_post):
    B, S, D = x.shape
    row = lambda a: a.reshape(1, -1)
    bf = lambda a: a.astype(_BF16)

    w_up, w_down = bf(mlp_w_up), bf(mlp_w_down)

    ca, cm, cp = _rope_tables(S)
    qt, k, vt = _qkv_rope_call(x, row(norm_mixer_pre[0]), attn_w_qkv, ca, cm, cp)
    o = _diff_attn_call(qt, k, vt, row(attn_lambda_q1[0]), row(attn_lambda_k1[0]),
                        row(attn_lambda_q2[0]), row(attn_lambda_k2[0]),
                        row(attn_subln_g[0]), _lambda_init(0))
    x2 = _attn_out_mlp_call(
        o.reshape(B * S, D), x.reshape(B * S, D), bf(attn_w_o[0]),
        row(norm_mixer_post[0]), row(norm_mlp_pre[0]), w_up, w_down,
        row(norm_mlp_post[0]), layer=0)

    x3 = _conv_mlp_call(
        x2, S, row(norm_mixer_pre[1]), bf(conv_w_in[0]), conv_w[0], bf(conv_w_out[0]),
        row(norm_mixer_post[1]), row(norm_mlp_pre[1]), w_up, w_down,
        row(norm_mlp_post[1]), layer=1)
    return x3.reshape(B, S, D)
```

```python
import functools
import math

import jax
import jax.numpy as jnp
import numpy as np
from jax import lax
from jax.experimental import pallas as pl
from jax.experimental.pallas import tpu as pltpu

D_MODEL = 1024
HEAD_DIM = 64
V_DIM = 2 * HEAD_DIM
N_HEADS = D_MODEL // V_DIM
QK_WIDTH = N_HEADS * 2 * HEAD_DIM
ROT_DIM = HEAD_DIM // 4
ROPE_THETA = 500000.0
CONV_WIDTH = 3
D_FF = 4 * D_MODEL
EPS = 1e-6

V7X_LANES = 128
V7X_SUBLANES = 8
V7X_MXU_DIM = 256
V7X_VMEM_LIMIT_BYTES = 56 * 1024 * 1024

TOKEN_TILE = 512
ATTN_K_TILE = 512
ATTN_Q_TILE = 1024
SUM_ROWS = 16
FF_CHUNK = 1024
LOG2E = 1.4426950408889634

_BF16 = jnp.bfloat16
_F32 = jnp.float32


def _lambda_init(layer_idx):
    return 0.8 - 0.6 * math.exp(-0.3 * layer_idx)


def _rms(x, g):
    return x * lax.rsqrt(jnp.mean(x * x, axis=-1, keepdims=True) + EPS) * g


def _dot(a, b):
    return jnp.dot(a, b, preferred_element_type=_F32)


def _resident(shape, layer=None):
    if layer is None:
        return pl.BlockSpec(shape, lambda *_: (0,) * len(shape),
                            pipeline_mode=pl.Buffered(1))
    return pl.BlockSpec((None,) + shape, lambda *_: (layer,) + (0,) * len(shape),
                        pipeline_mode=pl.Buffered(1))


def _qkv_rope_kernel(x_ref, g_ref, w_ref, ca_ref, cm_ref, cp_ref,
                     qt_ref, k_ref, vt_ref):
    qkv = _dot(_rms(x_ref[...], g_ref[...]), w_ref[...])
    ca, cm, cp = ca_ref[...], cm_ref[...], cp_ref[...]

    def rope(t):
        return (t * ca + pltpu.roll(t, V7X_LANES - ROT_DIM // 2, 1) * cm
                + pltpu.roll(t, ROT_DIM // 2, 1) * cp)

    q_scale = (HEAD_DIM ** -0.5) * LOG2E
    for hd in range(N_HEADS):
        lo, hi = hd * V_DIM, (hd + 1) * V_DIM
        qt_ref[lo:hi, :] = (rope(qkv[:, lo:hi]) * q_scale).T.astype(_BF16)
        k_ref[:, lo:hi] = rope(qkv[:, QK_WIDTH + lo:QK_WIDTH + hi]).astype(_BF16)
        vt_ref[lo:hi, :] = qkv[:, 2 * QK_WIDTH + lo:2 * QK_WIDTH + hi].T.astype(_BF16)


def _qkv_rope_call(x, g, w_qkv, ca, cm, cp):
    B, S, D = x.shape
    tm = TOKEN_TILE
    n_w = w_qkv.shape[-1]
    return pl.pallas_call(
        _qkv_rope_kernel,
        grid=(B, S // tm),
        in_specs=[
            pl.BlockSpec((None, tm, D), lambda b, i: (b, i, 0)),
            _resident((1, D)),
            _resident((D, n_w), layer=0),
            pl.BlockSpec((tm, V_DIM), lambda b, i: (i, 0)),
            pl.BlockSpec((tm, V_DIM), lambda b, i: (i, 0)),
            pl.BlockSpec((tm, V_DIM), lambda b, i: (i, 0)),
        ],
        out_specs=[
            pl.BlockSpec((None, QK_WIDTH, tm), lambda b, i: (b, 0, i)),
            pl.BlockSpec((None, tm, QK_WIDTH), lambda b, i: (b, i, 0)),
            pl.BlockSpec((None, D, tm), lambda b, i: (b, 0, i)),
        ],
        out_shape=[
            jax.ShapeDtypeStruct((B, QK_WIDTH, S), _BF16),
            jax.ShapeDtypeStruct((B, S, QK_WIDTH), _BF16),
            jax.ShapeDtypeStruct((B, D, S), _BF16),
        ],
        compiler_params=pltpu.CompilerParams(
            dimension_semantics=("arbitrary", "arbitrary"),
            vmem_limit_bytes=V7X_VMEM_LIMIT_BYTES),
        name="qkv_rope",
    )(x, g, w_qkv, ca, cm, cp)


def _diff_attn_kernel(par_ref, qt_ref, k_ref, vt_ref, o_ref,
                      qz_ref, s_ref, x_ref, m_ref, acc_ref, *, lambda_init):
    tq, tk = ATTN_Q_TILE, ATTN_K_TILE
    ns = V7X_SUBLANES
    i = pl.program_id(2)
    blk = V7X_MXU_DIM

    m_ref[...] = jnp.full(m_ref.shape, -jnp.inf, _F32)
    acc_ref[...] = jnp.zeros(acc_ref.shape, _F32)

    qt = qt_ref[...]
    sub = lax.broadcasted_iota(jnp.int32, qt.shape, 0) // HEAD_DIM
    for c in range(2):
        qz_ref[c] = jnp.where(sub == c, qt, jnp.zeros_like(qt))

    def live_keys(n, key_off):
        return tk if key_off is None else min(tk, n + blk - key_off)

    def score_block(j, slot, c, n, key_off):
        cols = slice(n, n + blk)
        nk = live_keys(n, key_off)
        k = k_ref[pl.ds(pl.multiple_of(j * tk, tk), nk), :]
        s = _dot(k, qz_ref[c, :, cols])
        if key_off is not None:
            key = lax.broadcasted_iota(jnp.int32, (nk, blk), 0) + key_off
            qry = lax.broadcasted_iota(jnp.int32, (nk, blk), 1) + n
            s = jnp.where(key <= qry, s, -1e30)
        s_ref[slot, c, 0:nk, cols] = s
        x_ref[slot, c, :, cols] = jnp.max(s.reshape(nk // ns, ns, blk), axis=0)

    def consume_block(j, slot, c, n, key_off):
        cols = slice(n, n + blk)
        m_old = m_ref[c, :, cols]
        m_new = jnp.maximum(m_old, jnp.max(x_ref[slot, c, :, cols], axis=0, keepdims=True))
        m_ref[c, :, cols] = m_new
        pv = None
        for kh in range(live_keys(n, key_off) // blk):
            keys = slice(kh * blk, (kh + 1) * blk)
            vt = vt_ref[:, pl.ds(pl.multiple_of(j * tk + kh * blk, blk), blk)]
            vt1 = jnp.concatenate([vt, jnp.ones((SUM_ROWS, blk), _BF16)], axis=0)
            p = jnp.exp2(s_ref[slot, c, keys, cols] - m_new).astype(_BF16)
            part = _dot(vt1, p)
            pv = part if pv is None else pv + part
        acc_ref[c, :, cols] = jnp.exp2(m_old - m_new) * acc_ref[c, :, cols] + pv

    def stage(score=None, consume=None):
        for c in range(2):
            for n in range(0, tq, blk):
                if consume is not None and n >= consume[2]:
                    consume_block(consume[0], consume[1], c, n, consume[3])
                if score is not None and n >= score[2]:
                    score_block(score[0], score[1], c, n, score[3])

    last = 2 * i + 1

    @pl.when(i == 0)
    def _():
        stage(score=(0, 0, 0, 0))
        stage(score=(1, 1, tk, tk), consume=(0, 0, 0, 0))
        stage(consume=(1, 1, tk, tk))

    @pl.when(i > 0)
    def _():
        stage(score=(0, 0, 0, None))
        stage(score=(1, 1, 0, None), consume=(0, 0, 0, None))

        def pair(jj, carry):
            j = 2 * jj + 1
            stage(score=(j + 1, 0, 0, None), consume=(j, 1, 0, None))
            stage(score=(j + 2, 1, 0, None), consume=(j + 1, 0, 0, None))
            return carry

        lax.fori_loop(0, i - 1, pair, 0)

        stage(score=(last - 1, 0, 0, 0), consume=(last - 2, 1, 0, None))
        stage(score=(last, 1, tk, tk), consume=(last - 1, 0, 0, 0))
        stage(consume=(last, 1, tk, tk))

    lam = (jnp.exp(jnp.sum(par_ref[0:1, :] * par_ref[1:2, :]))
           - jnp.exp(jnp.sum(par_ref[2:3, :] * par_ref[3:4, :])) + lambda_init)
    l1 = acc_ref[0, V_DIM:V_DIM + 1, :]
    l2 = acc_ref[1, V_DIM:V_DIM + 1, :]
    ot = acc_ref[0, :V_DIM, :] / l1 - lam * (acc_ref[1, :V_DIM, :] / l2)
    ot = ot * lax.rsqrt(jnp.mean(ot * ot, axis=0, keepdims=True) + EPS)
    o = ot.T * par_ref[4:5, :] * (1.0 - lambda_init)
    o_ref[...] = o.astype(o_ref.dtype)


def _diff_attn_call(qt, k, vt, lq1, lk1, lq2, lk2, g, lambda_init):
    B, S, _ = k.shape
    tq, tk = ATTN_Q_TILE, ATTN_K_TILE
    pad = lambda a: jnp.pad(a.reshape(1, -1), ((0, 0), (0, V7X_LANES - a.size)))
    par = jnp.concatenate([pad(lq1), pad(lk1), pad(lq2), pad(lk2), pad(g),
                           jnp.zeros((V7X_SUBLANES - 5, V7X_LANES), _F32)], axis=0)
    return pl.pallas_call(
        functools.partial(_diff_attn_kernel, lambda_init=lambda_init),
        grid=(B, N_HEADS, S // tq),
        in_specs=[
            _resident((V7X_SUBLANES, V7X_LANES)),
            pl.BlockSpec((None, V_DIM, tq), lambda b, h, i: (b, h, i)),
            pl.BlockSpec((None, S, V_DIM), lambda b, h, i: (b, 0, h)),
            pl.BlockSpec((None, V_DIM, S), lambda b, h, i: (b, h, 0)),
        ],
        out_specs=pl.BlockSpec((None, tq, V_DIM), lambda b, h, i: (b, i, h)),
        out_shape=jax.ShapeDtypeStruct((B, S, N_HEADS * V_DIM), _BF16),
        scratch_shapes=[
            pltpu.VMEM((2, V_DIM, tq), _BF16),
            pltpu.VMEM((2, 2, tk, tq), _F32),
            pltpu.VMEM((2, 2, V7X_SUBLANES, tq), _F32),
            pltpu.VMEM((2, 1, tq), _F32),
            pltpu.VMEM((2, V_DIM + SUM_ROWS, tq), _F32),
        ],
        compiler_params=pltpu.CompilerParams(
            dimension_semantics=("arbitrary", "arbitrary", "arbitrary"),
            vmem_limit_bytes=V7X_VMEM_LIMIT_BYTES),
        name="diff_attn",
    )(par, qt, k, vt)


def _mixer_out_mlp(mix_ref, w_mix_ref, x_ref, g_post_ref, g_mpre_ref, w_up_ref, w_down_ref,
                   g_mpost_ref, out_ref, act_ref):
    hr = x_ref.shape[0] // 2
    halves = (slice(0, hr), slice(hr, 2 * hr))
    ms = [_dot(mix_ref[r, :], w_mix_ref[...]) for r in halves]
    for r, m in zip(halves, ms):
        x1 = x_ref[r, :] + _rms(m, g_post_ref[...])
        out_ref[r, :] = x1
        h = _rms(x1, g_mpre_ref[...]).astype(_BF16)
        for c in range(D_FF // FF_CHUNK):
            lo, hi = c * FF_CHUNK, (c + 1) * FF_CHUNK
            u = jnp.maximum(_dot(h, w_up_ref[:, lo:hi]), 0.0)
            act_ref[r, lo:hi] = (u * u).astype(_BF16)
    ys = [_dot(act_ref[r, :], w_down_ref[...]) for r in halves]
    for r, y in zip(halves, ys):
        out_ref[r, :] = out_ref[r, :] + _rms(y, g_mpost_ref[...])


def _attn_out_mlp_kernel(o_ref, x_ref, w_o_ref, g_post_ref, g_mpre_ref,
                         w_up_ref, w_down_ref, g_mpost_ref, out_ref, act_ref):
    _mixer_out_mlp(o_ref, w_o_ref, x_ref, g_post_ref, g_mpre_ref, w_up_ref, w_down_ref,
                   g_mpost_ref, out_ref, act_ref)


def _attn_out_mlp_call(o, x, w_o, g_post, g_mpre, w_up, w_down, g_mpost, layer):
    T, D = x.shape
    tm = TOKEN_TILE
    row = lambda i: (i, 0)
    return pl.pallas_call(
        _attn_out_mlp_kernel,
        grid=(T // tm,),
        in_specs=[
            pl.BlockSpec((tm, D), row),
            pl.BlockSpec((tm, D), row),
            _resident((D, D)),
            _resident((1, D)),
            _resident((1, D)),
            _resident((D, D_FF), layer),
            _resident((D_FF, D), layer),
            _resident((1, D)),
        ],
        out_specs=pl.BlockSpec((tm, D), row),
        out_shape=jax.ShapeDtypeStruct((T, D), _F32),
        scratch_shapes=[pltpu.VMEM((tm, D_FF), _BF16)],
        compiler_params=pltpu.CompilerParams(
            dimension_semantics=("arbitrary",),
            vmem_limit_bytes=V7X_VMEM_LIMIT_BYTES),
        name="attn_out_mlp",
    )(o, x, w_o, g_post, g_mpre, w_up, w_down, g_mpost)


def _conv_mlp_kernel(x_ref, g_pre_ref, w_in_ref, cw_ref, w_out_ref, g_post_ref,
                     g_mpre_ref, w_up_ref, w_down_ref, g_mpost_ref,
                     out_ref, act_ref, gated_ref, tail_ref, *, tiles_per_seq):
    D = D_MODEL
    x = x_ref[...]
    h = _rms(x, g_pre_ref[...]).astype(_BF16)
    hh = _dot(h, w_in_ref[...])
    b_gate = hh[:, :D]
    u = hh[:, D:2 * D] * hh[:, 2 * D:]

    @pl.when(pl.program_id(0) % tiles_per_seq == 0)
    def _():
        tail_ref[...] = jnp.zeros_like(tail_ref)

    w0, w1, w2 = cw_ref[0:1, :], cw_ref[1:2, :], cw_ref[2:3, :]
    y = w0 * pltpu.roll(u, 2, 0) + w1 * pltpu.roll(u, 1, 0) + w2 * u
    gated_ref[...] = (b_gate * y).astype(_BF16)

    ns = V7X_SUBLANES
    tail = tail_ref[...]
    u_top = u[:ns, :]
    rid = lax.broadcasted_iota(jnp.int32, (ns, D), 0)
    u1 = jnp.where(rid < 1, pltpu.roll(tail, 1, 0), pltpu.roll(u_top, 1, 0))
    u2 = jnp.where(rid < 2, pltpu.roll(tail, 2, 0), pltpu.roll(u_top, 2, 0))
    y_top = w0 * u2 + w1 * u1 + w2 * u_top
    gated_ref[:ns, :] = (b_gate[:ns, :] * y_top).astype(_BF16)
    tail_ref[...] = u[-ns:, :]

    _mixer_out_mlp(gated_ref, w_out_ref, x_ref, g_post_ref, g_mpre_ref, w_up_ref, w_down_ref,
                   g_mpost_ref, out_ref, act_ref)


def _conv_mlp_call(x, seq_len, g_pre, w_in, conv_w, w_out, g_post,
                   g_mpre, w_up, w_down, g_mpost, layer):
    T, D = x.shape
    tm = TOKEN_TILE
    row = lambda i: (i, 0)
    return pl.pallas_call(
        functools.partial(_conv_mlp_kernel, tiles_per_seq=seq_len // tm),
        grid=(T // tm,),
        in_specs=[
            pl.BlockSpec((tm, D), row),
            _resident((1, D)),
            _resident((D, 3 * D)),
            _resident((CONV_WIDTH, D)),
            _resident((D, D)),
            _resident((1, D)),
            _resident((1, D)),
            _resident((D, D_FF), layer),
            _resident((D_FF, D), layer),
            _resident((1, D)),
        ],
        out_specs=pl.BlockSpec((tm, D), row),
        out_shape=jax.ShapeDtypeStruct((T, D), _F32),
        scratch_shapes=[
            pltpu.VMEM((tm, D_FF), _BF16),
            pltpu.VMEM((tm, D), _BF16),
            pltpu.VMEM((V7X_SUBLANES, D), _F32),
        ],
        compiler_params=pltpu.CompilerParams(
            dimension_semantics=("arbitrary",),
            vmem_limit_bytes=V7X_VMEM_LIMIT_BYTES),
        name="conv_mlp",
    )(x, g_pre, w_in, conv_w, w_out, g_post, g_mpre, w_up, w_down, g_mpost)


def _rope_tables(seq_len):
    half = ROT_DIM // 2
    pos = np.arange(seq_len, dtype=np.float64)
    inv_freq = ROPE_THETA ** (-np.arange(0, ROT_DIM, 2, dtype=np.float64) / ROT_DIM)
    ang = pos[:, None] * inv_freq[None, :]
    cos = np.cos(ang).astype(np.float32)
    sin = np.sin(ang).astype(np.float32)
    ones = np.ones((seq_len, HEAD_DIM - ROT_DIM), np.float32)
    zeros = np.zeros((seq_len, HEAD_DIM - half), np.float32)
    zeros_h = np.zeros((seq_len, half), np.float32)
    zeros_r = np.zeros((seq_len, HEAD_DIM - ROT_DIM), np.float32)
    ca = np.concatenate([cos, cos, ones], axis=-1)
    cm = np.concatenate([-sin, zeros], axis=-1)
    cp = np.concatenate([zeros_h, sin, zeros_r], axis=-1)
    two = lambda a: jnp.asarray(np.concatenate([a, a], axis=-1))
    return two(ca), two(cm), two(cp)


def kernel(x, attn_w_qkv, attn_w_o, attn_lambda_q1, attn_lambda_k1, attn_lambda_q2,
           attn_lambda_k2, attn_subln_g, conv_w_in, conv_w, conv_w_out, mlp_w_up,
           mlp_w_down, norm_mixer_pre, norm_mixer_post, norm_mlp_pre, norm_mlp_post):
    B, S, D = x.shape
    row = lambda a: a.reshape(1, -1)
    bf = lambda a: a.astype(_BF16)

    w_up, w_down = bf(mlp_w_up), bf(mlp_w_down)

    ca, cm, cp = _rope_tables(S)
    qt, k, vt = _qkv_rope_call(x, row(norm_mixer_pre[0]), attn_w_qkv, ca, cm, cp)
    o = _diff_attn_call(qt, k, vt, row(attn_lambda_q1[0]), row(attn_lambda_k1[0]),
                        row(attn_lambda_q2[0]), row(attn_lambda_k2[0]),
                        row(attn_subln_g[0]), _lambda_init(0))
    x2 = _attn_out_mlp_call(
        o.reshape(B * S, D), x.reshape(B * S, D), bf(attn_w_o[0]),
        row(norm_mixer_post[0]), row(norm_mlp_pre[0]), w_up, w_down,
        row(norm_mlp_post[0]), layer=0)

    x3 = _conv_mlp_call(
        x2, S, row(norm_mixer_pre[1]), bf(conv_w_in[0]), conv_w[0], bf(conv_w_out[0]),
        row(norm_mixer_post[1]), row(norm_mlp_pre[1]), w_up, w_down,
        row(norm_mlp_post[1]), layer=1)
    return x3.reshape(B, S, D)
```

```python
import functools
import math

import jax
import jax.numpy as jnp
import numpy as np
from jax import lax
from jax.experimental import pallas as pl
from jax.experimental.pallas import tpu as pltpu

D_MODEL = 1024
HEAD_DIM = 64
V_DIM = 2 * HEAD_DIM
N_HEADS = D_MODEL // V_DIM
QK_WIDTH = N_HEADS * 2 * HEAD_DIM
ROT_DIM = HEAD_DIM // 4
ROPE_THETA = 500000.0
CONV_WIDTH = 3
D_FF = 4 * D_MODEL
EPS = 1e-6

V7X_LANES = 128
V7X_SUBLANES = 8
V7X_MXU_DIM = 256
V7X_VMEM_LIMIT_BYTES = 56 * 1024 * 1024

TOKEN_TILE = 512
ATTN_K_TILE = 512
ATTN_Q_TILE = 1024
SUM_ROWS = 16
FF_CHUNK = 1024
LOG2E = 1.4426950408889634

_BF16 = jnp.bfloat16
_F32 = jnp.float32


def _lambda_init(layer_idx):
    return 0.8 - 0.6 * math.exp(-0.3 * layer_idx)


def _rms(x, g):
    return x * lax.rsqrt(jnp.mean(x * x, axis=-1, keepdims=True) + EPS) * g


def _dot(a, b):
    return jnp.dot(a, b, preferred_element_type=_F32)


def _resident(shape, layer=None):
    if layer is None:
        return pl.BlockSpec(shape, lambda *_: (0,) * len(shape),
                            pipeline_mode=pl.Buffered(1))
    return pl.BlockSpec((None,) + shape, lambda *_: (layer,) + (0,) * len(shape),
                        pipeline_mode=pl.Buffered(1))


def _qkv_rope_kernel(x_ref, g_ref, w_ref, ca_ref, cm_ref, cp_ref, wu_ref, wd_ref,
                     qt_ref, k_ref, vt_ref, wu16_ref, wd16_ref):
    wu16_ref[...] = wu_ref[...].astype(_BF16)
    wd16_ref[...] = wd_ref[...].astype(_BF16)

    qkv = _dot(_rms(x_ref[...], g_ref[...]), w_ref[...])
    ca, cm, cp = ca_ref[...], cm_ref[...], cp_ref[...]

    def rope(t):
        return (t * ca + pltpu.roll(t, V7X_LANES - ROT_DIM // 2, 1) * cm
                + pltpu.roll(t, ROT_DIM // 2, 1) * cp)

    q_scale = (HEAD_DIM ** -0.5) * LOG2E
    for hd in range(N_HEADS):
        lo, hi = hd * V_DIM, (hd + 1) * V_DIM
        qt_ref[lo:hi, :] = (rope(qkv[:, lo:hi]) * q_scale).T.astype(_BF16)
        k_ref[:, lo:hi] = rope(qkv[:, QK_WIDTH + lo:QK_WIDTH + hi]).astype(_BF16)
        vt_ref[lo:hi, :] = qkv[:, 2 * QK_WIDTH + lo:2 * QK_WIDTH + hi].T.astype(_BF16)


def _qkv_rope_call(x, g, w_qkv, ca, cm, cp, w_up, w_down):
    B, S, D = x.shape
    tm = TOKEN_TILE
    n_w = w_qkv.shape[-1]
    n_i = S // tm
    wu = w_up.reshape(-1, w_up.shape[-1])
    wd = w_down.reshape(-1, w_down.shape[-1])
    ru, rd = wu.shape[0] // (B * n_i), wd.shape[0] // (B * n_i)
    slab = lambda b, i: (b * n_i + i, 0)
    qt, k, vt, wu16, wd16 = pl.pallas_call(
        _qkv_rope_kernel,
        grid=(B, n_i),
        in_specs=[
            pl.BlockSpec((None, tm, D), lambda b, i: (b, i, 0)),
            _resident((1, D)),
            _resident((D, n_w), layer=0),
            pl.BlockSpec((tm, V_DIM), lambda b, i: (i, 0)),
            pl.BlockSpec((tm, V_DIM), lambda b, i: (i, 0)),
            pl.BlockSpec((tm, V_DIM), lambda b, i: (i, 0)),
            pl.BlockSpec((ru, wu.shape[1]), slab),
            pl.BlockSpec((rd, wd.shape[1]), slab),
        ],
        out_specs=[
            pl.BlockSpec((None, QK_WIDTH, tm), lambda b, i: (b, 0, i)),
            pl.BlockSpec((None, tm, QK_WIDTH), lambda b, i: (b, i, 0)),
            pl.BlockSpec((None, D, tm), lambda b, i: (b, 0, i)),
            pl.BlockSpec((ru, wu.shape[1]), slab),
            pl.BlockSpec((rd, wd.shape[1]), slab),
        ],
        out_shape=[
            jax.ShapeDtypeStruct((B, QK_WIDTH, S), _BF16),
            jax.ShapeDtypeStruct((B, S, QK_WIDTH), _BF16),
            jax.ShapeDtypeStruct((B, D, S), _BF16),
            jax.ShapeDtypeStruct(wu.shape, _BF16),
            jax.ShapeDtypeStruct(wd.shape, _BF16),
        ],
        compiler_params=pltpu.CompilerParams(
            dimension_semantics=("arbitrary", "arbitrary"),
            vmem_limit_bytes=V7X_VMEM_LIMIT_BYTES),
        name="qkv_rope",
    )(x, g, w_qkv, ca, cm, cp, wu, wd)
    return qt, k, vt, wu16.reshape(w_up.shape), wd16.reshape(w_down.shape)


def _diff_attn_kernel(par_ref, qt_ref, k_ref, vt_ref, o_ref,
                      qz_ref, s_ref, x_ref, m_ref, acc_ref, *, lambda_init):
    tq, tk = ATTN_Q_TILE, ATTN_K_TILE
    ns = V7X_SUBLANES
    i = pl.program_id(2)
    blk = V7X_MXU_DIM

    qt = qt_ref[...]
    sub = lax.broadcasted_iota(jnp.int32, qt.shape, 0) // HEAD_DIM
    for c in range(2):
        qz_ref[c] = jnp.where(sub == c, qt, jnp.zeros_like(qt))

    def live_keys(n, key_off):
        return tk if key_off is None else min(tk, n + blk - key_off)

    def score_block(j, slot, c, n, key_off):
        cols = slice(n, n + blk)
        nk = live_keys(n, key_off)
        k = k_ref[pl.ds(pl.multiple_of(j * tk, tk), nk), :]
        s = _dot(k, qz_ref[c, :, cols])
        if key_off is not None:
            key = lax.broadcasted_iota(jnp.int32, (nk, blk), 0) + key_off
            qry = lax.broadcasted_iota(jnp.int32, (nk, blk), 1) + n
            s = jnp.where(key <= qry, s, -1e30)
        s_ref[slot, c, 0:nk, cols] = s
        x_ref[slot, c, :, cols] = jnp.max(s.reshape(nk // ns, ns, blk), axis=0)

    def consume_block(j, slot, c, n, key_off, first):
        cols = slice(n, n + blk)
        m_new = jnp.max(x_ref[slot, c, :, cols], axis=0, keepdims=True)
        if not first:
            m_old = m_ref[c, :, cols]
            m_new = jnp.maximum(m_old, m_new)
        m_ref[c, :, cols] = m_new
        pv = None
        for kh in range(live_keys(n, key_off) // blk):
            keys = slice(kh * blk, (kh + 1) * blk)
            vt = vt_ref[:, pl.ds(pl.multiple_of(j * tk + kh * blk, blk), blk)]
            vt1 = jnp.concatenate([vt, jnp.ones((SUM_ROWS, blk), _BF16)], axis=0)
            p = jnp.exp2(s_ref[slot, c, keys, cols] - m_new).astype(_BF16)
            part = _dot(vt1, p)
            pv = part if pv is None else pv + part
        if first:
            acc_ref[c, :, cols] = pv
        else:
            acc_ref[c, :, cols] = jnp.exp2(m_old - m_new) * acc_ref[c, :, cols] + pv

    def stage(score=None, consume=None, first=False):
        for c in range(2):
            for n in range(0, tq, blk):
                if consume is not None and n >= consume[2]:
                    consume_block(consume[0], consume[1], c, n, consume[3], first)
                if score is not None and n >= score[2]:
                    score_block(score[0], score[1], c, n, score[3])

    last = 2 * i + 1

    @pl.when(i == 0)
    def _():
        stage(score=(0, 0, 0, 0))
        stage(score=(1, 1, tk, tk), consume=(0, 0, 0, 0), first=True)
        stage(consume=(1, 1, tk, tk))

    @pl.when(i > 0)
    def _():
        stage(score=(0, 0, 0, None))
        stage(score=(1, 1, 0, None), consume=(0, 0, 0, None), first=True)

        def pair(jj, carry):
            j = 2 * jj + 1
            stage(score=(j + 1, 0, 0, None), consume=(j, 1, 0, None))
            stage(score=(j + 2, 1, 0, None), consume=(j + 1, 0, 0, None))
            return carry

        lax.fori_loop(0, i - 1, pair, 0)

        stage(score=(last - 1, 0, 0, 0), consume=(last - 2, 1, 0, None))
        stage(score=(last, 1, tk, tk), consume=(last - 1, 0, 0, 0))
        stage(consume=(last, 1, tk, tk))

    lam = (jnp.exp(jnp.sum(par_ref[0:1, :] * par_ref[1:2, :]))
           - jnp.exp(jnp.sum(par_ref[2:3, :] * par_ref[3:4, :])) + lambda_init)
    l1 = acc_ref[0, V_DIM:V_DIM + 1, :]
    l2 = acc_ref[1, V_DIM:V_DIM + 1, :]
    ot = acc_ref[0, :V_DIM, :] / l1 - lam * (acc_ref[1, :V_DIM, :] / l2)
    ot = ot * lax.rsqrt(jnp.mean(ot * ot, axis=0, keepdims=True) + EPS)
    o = ot.T * par_ref[4:5, :] * (1.0 - lambda_init)
    o_ref[...] = o.astype(o_ref.dtype)


def _diff_attn_call(qt, k, vt, lq1, lk1, lq2, lk2, g, lambda_init):
    B, S, _ = k.shape
    tq, tk = ATTN_Q_TILE, ATTN_K_TILE
    pad = lambda a: jnp.pad(a.reshape(1, -1), ((0, 0), (0, V7X_LANES - a.size)))
    par = jnp.concatenate([pad(lq1), pad(lk1), pad(lq2), pad(lk2), pad(g),
                           jnp.zeros((V7X_SUBLANES - 5, V7X_LANES), _F32)], axis=0)
    return pl.pallas_call(
        functools.partial(_diff_attn_kernel, lambda_init=lambda_init),
        grid=(B, N_HEADS, S // tq),
        in_specs=[
            _resident((V7X_SUBLANES, V7X_LANES)),
            pl.BlockSpec((None, V_DIM, tq), lambda b, h, i: (b, h, i)),
            pl.BlockSpec((None, S, V_DIM), lambda b, h, i: (b, 0, h)),
            pl.BlockSpec((None, V_DIM, S), lambda b, h, i: (b, h, 0)),
        ],
        out_specs=pl.BlockSpec((None, tq, V_DIM), lambda b, h, i: (b, i, h)),
        out_shape=jax.ShapeDtypeStruct((B, S, N_HEADS * V_DIM), _BF16),
        scratch_shapes=[
            pltpu.VMEM((2, V_DIM, tq), _BF16),
            pltpu.VMEM((2, 2, tk, tq), _F32),
            pltpu.VMEM((2, 2, V7X_SUBLANES, tq), _F32),
            pltpu.VMEM((2, 1, tq), _F32),
            pltpu.VMEM((2, V_DIM + SUM_ROWS, tq), _F32),
        ],
        compiler_params=pltpu.CompilerParams(
            dimension_semantics=("arbitrary", "arbitrary", "arbitrary"),
            vmem_limit_bytes=V7X_VMEM_LIMIT_BYTES),
        name="diff_attn",
    )(par, qt, k, vt)


def _mixer_out_mlp(mix_ref, w_mix_ref, x_ref, g_post_ref, g_mpre_ref, w_up_ref, w_down_ref,
                   g_mpost_ref, out_ref, act_ref):
    hr = x_ref.shape[0] // 2
    halves = (slice(0, hr), slice(hr, 2 * hr))
    ms = [_dot(mix_ref[r, :], w_mix_ref[...]) for r in halves]
    for r, m in zip(halves, ms):
        x1 = x_ref[r, :] + _rms(m, g_post_ref[...])
        out_ref[r, :] = x1
        h = _rms(x1, g_mpre_ref[...]).astype(_BF16)
        for c in range(D_FF // FF_CHUNK):
            lo, hi = c * FF_CHUNK, (c + 1) * FF_CHUNK
            u = jnp.maximum(_dot(h, w_up_ref[:, lo:hi]), 0.0)
            act_ref[r, lo:hi] = (u * u).astype(_BF16)
    ys = [_dot(act_ref[r, :], w_down_ref[...]) for r in halves]
    for r, y in zip(halves, ys):
        out_ref[r, :] = out_ref[r, :] + _rms(y, g_mpost_ref[...])


def _attn_out_mlp_kernel(o_ref, x_ref, w_o_ref, g_post_ref, g_mpre_ref,
                         w_up_ref, w_down_ref, g_mpost_ref, out_ref, act_ref):
    _mixer_out_mlp(o_ref, w_o_ref, x_ref, g_post_ref, g_mpre_ref, w_up_ref, w_down_ref,
                   g_mpost_ref, out_ref, act_ref)


def _attn_out_mlp_call(o, x, w_o, g_post, g_mpre, w_up, w_down, g_mpost, layer):
    T, D = x.shape
    tm = TOKEN_TILE
    row = lambda i: (i, 0)
    return pl.pallas_call(
        _attn_out_mlp_kernel,
        grid=(T // tm,),
        in_specs=[
            pl.BlockSpec((tm, D), row),
            pl.BlockSpec((tm, D), row),
            _resident((D, D)),
            _resident((1, D)),
            _resident((1, D)),
            _resident((D, D_FF), layer),
            _resident((D_FF, D), layer),
            _resident((1, D)),
        ],
        out_specs=pl.BlockSpec((tm, D), row),
        out_shape=jax.ShapeDtypeStruct((T, D), _F32),
        scratch_shapes=[pltpu.VMEM((tm, D_FF), _BF16)],
        compiler_params=pltpu.CompilerParams(
            dimension_semantics=("arbitrary",),
            vmem_limit_bytes=V7X_VMEM_LIMIT_BYTES),
        name="attn_out_mlp",
    )(o, x, w_o, g_post, g_mpre, w_up, w_down, g_mpost)


def _conv_mlp_kernel(x_ref, g_pre_ref, w_in_ref, cw_ref, w_out_ref, g_post_ref,
                     g_mpre_ref, w_up_ref, w_down_ref, g_mpost_ref,
                     out_ref, act_ref, gated_ref, tail_ref, *, tiles_per_seq):
    D = D_MODEL
    x = x_ref[...]
    hh = _dot(_rms(x, g_pre_ref[...]), w_in_ref[...])
    b_gate = hh[:, :D]
    u = hh[:, D:2 * D] * hh[:, 2 * D:]

    @pl.when(pl.program_id(0) % tiles_per_seq == 0)
    def _():
        tail_ref[...] = jnp.zeros_like(tail_ref)

    w0, w1, w2 = cw_ref[0:1, :], cw_ref[1:2, :], cw_ref[2:3, :]
    y = w0 * pltpu.roll(u, 2, 0) + w1 * pltpu.roll(u, 1, 0) + w2 * u
    gated_ref[...] = (b_gate * y).astype(_BF16)

    ns = V7X_SUBLANES
    tail = tail_ref[...]
    u_top = u[:ns, :]
    rid = lax.broadcasted_iota(jnp.int32, (ns, D), 0)
    u1 = jnp.where(rid < 1, pltpu.roll(tail, 1, 0), pltpu.roll(u_top, 1, 0))
    u2 = jnp.where(rid < 2, pltpu.roll(tail, 2, 0), pltpu.roll(u_top, 2, 0))
    y_top = w0 * u2 + w1 * u1 + w2 * u_top
    gated_ref[:ns, :] = (b_gate[:ns, :] * y_top).astype(_BF16)
    tail_ref[...] = u[-ns:, :]

    _mixer_out_mlp(gated_ref, w_out_ref, x_ref, g_post_ref, g_mpre_ref, w_up_ref, w_down_ref,
                   g_mpost_ref, out_ref, act_ref)


def _conv_mlp_call(x, seq_len, g_pre, w_in, conv_w, w_out, g_post,
                   g_mpre, w_up, w_down, g_mpost, layer):
    T, D = x.shape
    tm = TOKEN_TILE
    row = lambda i: (i, 0)
    return pl.pallas_call(
        functools.partial(_conv_mlp_kernel, tiles_per_seq=seq_len // tm),
        grid=(T // tm,),
        in_specs=[
            pl.BlockSpec((tm, D), row),
            _resident((1, D)),
            _resident((D, 3 * D), layer=0),
            _resident((CONV_WIDTH, D)),
            _resident((D, D)),
            _resident((1, D)),
            _resident((1, D)),
            _resident((D, D_FF), layer),
            _resident((D_FF, D), layer),
            _resident((1, D)),
        ],
        out_specs=pl.BlockSpec((tm, D), row),
        out_shape=jax.ShapeDtypeStruct((T, D), _F32),
        scratch_shapes=[
            pltpu.VMEM((tm, D_FF), _BF16),
            pltpu.VMEM((tm, D), _BF16),
            pltpu.VMEM((V7X_SUBLANES, D), _F32),
        ],
        compiler_params=pltpu.CompilerParams(
            dimension_semantics=("arbitrary",),
            vmem_limit_bytes=V7X_VMEM_LIMIT_BYTES),
        name="conv_mlp",
    )(x, g_pre, w_in, conv_w, w_out, g_post, g_mpre, w_up, w_down, g_mpost)


def _rope_tables(seq_len):
    half = ROT_DIM // 2
    pos = np.arange(seq_len, dtype=np.float64)
    inv_freq = ROPE_THETA ** (-np.arange(0, ROT_DIM, 2, dtype=np.float64) / ROT_DIM)
    ang = pos[:, None] * inv_freq[None, :]
    cos = np.cos(ang).astype(np.float32)
    sin = np.sin(ang).astype(np.float32)
    ones = np.ones((seq_len, HEAD_DIM - ROT_DIM), np.float32)
    zeros = np.zeros((seq_len, HEAD_DIM - half), np.float32)
    zeros_h = np.zeros((seq_len, half), np.float32)
    zeros_r = np.zeros((seq_len, HEAD_DIM - ROT_DIM), np.float32)
    ca = np.concatenate([cos, cos, ones], axis=-1)
    cm = np.concatenate([-sin, zeros], axis=-1)
    cp = np.concatenate([zeros_h, sin, zeros_r], axis=-1)
    two = lambda a: jnp.asarray(np.concatenate([a, a], axis=-1))
    return two(ca), two(cm), two(cp)


def kernel(x, attn_w_qkv, attn_w_o, attn_lambda_q1, attn_lambda_k1, attn_lambda_q2,
           attn_lambda_k2, attn_subln_g, conv_w_in, conv_w, conv_w_out, mlp_w_up,
           mlp_w_down, norm_mixer_pre, norm_mixer_post, norm_mlp_pre, norm_mlp_post):
    B, S, D = x.shape
    row = lambda a: a.reshape(1, -1)
    bf = lambda a: a.astype(_BF16)

    ca, cm, cp = _rope_tables(S)
    qt, k, vt, w_up, w_down = _qkv_rope_call(x, row(norm_mixer_pre[0]), attn_w_qkv, ca, cm, cp,
                                             mlp_w_up, mlp_w_down)
    o = _diff_attn_call(qt, k, vt, row(attn_lambda_q1[0]), row(attn_lambda_k1[0]),
                        row(attn_lambda_q2[0]), row(attn_lambda_k2[0]),
                        row(attn_subln_g[0]), _lambda_init(0))
    x2 = _attn_out_mlp_call(
        o.reshape(B * S, D), x.reshape(B * S, D), bf(attn_w_o[0]),
        row(norm_mixer_post[0]), row(norm_mlp_pre[0]), w_up, w_down,
        row(norm_mlp_post[0]), layer=0)

    x3 = _conv_mlp_call(
        x2, S, row(norm_mixer_pre[1]), conv_w_in, conv_w[0], bf(conv_w_out[0]),
        row(norm_mixer_post[1]), row(norm_mlp_pre[1]), w_up, w_down,
        row(norm_mlp_post[1]), layer=1)
    return x3.reshape(B, S, D)
```
